```python
import math
import jax, jax.numpy as jnp
from jax import lax
import numpy as np

D_MODEL = 1024
BATCH = 4
SEQ = 4096
DEPTH = 2

GLA_HEADS = 4
GLA_DK = 64
GLA_DV = 128
GLA_KEY = GLA_HEADS * GLA_DK
GLA_VAL = GLA_HEADS * GLA_DV
GATE_RANK = 16
GATE_NORMALIZER = 16.0
CHUNK = 64
CONV_CH = 512
CONV_WIDTH = 31
MIX_WIDTH = GLA_VAL + CONV_CH
IN_COLS = 2 * GLA_KEY + 2 * GLA_VAL + GATE_RANK + 2 * CONV_CH
D_FF = 2816
N_EXPERTS = 8
TOP_K = 2
N_DENSE = (DEPTH + 1) // 2
N_MOE = DEPTH // 2
EPS = 1e-6

kernel_name = "hybrid_gla_conformer_moe_trunk"


def rmsnorm(x, w):
    xf = x.astype(jnp.float32)
    y = xf * lax.rsqrt(jnp.mean(xf * xf, axis=-1, keepdims=True) + EPS)
    return (y * w.astype(jnp.float32)).astype(x.dtype)


def layernorm(x, w, b):
    xf = x.astype(jnp.float32)
    mu = jnp.mean(xf, axis=-1, keepdims=True)
    var = jnp.mean(jnp.square(xf - mu), axis=-1, keepdims=True)
    y = (xf - mu) * lax.rsqrt(var + EPS)
    return (y * w.astype(jnp.float32) + b.astype(jnp.float32)).astype(x.dtype)


def gla_chunked(q, k, v, log_a):
    B, H, T, DK = q.shape
    DV = v.shape[-1]
    N = T // CHUNK

    def to_chunks(a):
        return a.astype(jnp.float32).reshape(B, H, N, CHUNK, a.shape[-1]).transpose(2, 0, 1, 3, 4)

    qc, kc, vc, ac = to_chunks(q), to_chunks(k), to_chunks(v), to_chunks(log_a)
    bc = jnp.cumsum(ac, axis=3)
    mask = jnp.tril(jnp.ones((CHUNK, CHUNK), dtype=bool))

    def step(S, inp):
        qi, ki, vi, bi = inp
        diff = bi[:, :, :, None, :] - bi[:, :, None, :, :]
        decay = jnp.exp(jnp.where(mask[:, :, None], diff, -jnp.inf))
        attn = jnp.einsum('bhid,bhjd,bhijd->bhij', qi, ki, decay)
        o_intra = jnp.einsum('bhij,bhjv->bhiv', attn, vi)
        o_inter = jnp.einsum('bhid,bhdv->bhiv', qi * jnp.exp(bi), S)
        b_last = bi[:, :, -1:, :]
        S_new = jnp.exp(b_last[:, :, 0, :])[..., None] * S + jnp.einsum(
            'bhjd,bhjv->bhdv', ki * jnp.exp(b_last - bi), vi)
        return S_new, o_intra + o_inter

    S0 = jnp.zeros((B, H, DK, DV), jnp.float32)
    _, ys = lax.scan(step, S0, (qc, kc, vc, bc))
    return ys.transpose(1, 2, 0, 3, 4).reshape(B, H, T, DV)


def causal_dwconv(u, w, b):
    kern = w[:, None, :].astype(u.dtype)
    y = lax.conv_general_dilated(u, kern, window_strides=(1,), padding=[(CONV_WIDTH - 1, 0)],
                                 dimension_numbers=('NWC', 'WIO', 'NWC'), feature_group_count=CONV_CH)
    return y + b.astype(u.dtype)


def swiglu(h, wg, wu, wd):
    return (jax.nn.silu(h @ wg) * (h @ wu)) @ wd


def token_mixer(h, w_in, w_gate_up, b_gate, gla_norm_w, conv_w, conv_b, cn_w, cn_b, w_out):
    B, T, _ = h.shape
    z = h @ w_in
    sizes = [GLA_KEY, GLA_KEY, GLA_VAL, GLA_VAL, GATE_RANK, CONV_CH, CONV_CH]
    offs = np.cumsum([0] + sizes)
    q, k, v, g, gr, ca, cb = [z[..., offs[i]:offs[i + 1]] for i in range(len(sizes))]

    gate_logit = (gr @ w_gate_up + b_gate).astype(jnp.float32)
    log_a = jax.nn.log_sigmoid(gate_logit) / GATE_NORMALIZER

    def heads(a, d):
        return a.reshape(B, T, GLA_HEADS, d).transpose(0, 2, 1, 3)

    o = gla_chunked(heads(q, GLA_DK) * (GLA_DK ** -0.5), heads(k, GLA_DK), heads(v, GLA_DV),
                    heads(log_a, GLA_DK))
    o = o.transpose(0, 2, 1, 3).astype(h.dtype)
    o = rmsnorm(o, gla_norm_w)
    o = (o * jax.nn.silu(g.reshape(B, T, GLA_HEADS, GLA_DV))).reshape(B, T, GLA_VAL)

    u = ca * jax.nn.sigmoid(cb)
    u = causal_dwconv(u, conv_w, conv_b)
    u = jax.nn.silu(layernorm(u, cn_w, cn_b))

    return jnp.concatenate([o, u], axis=-1) @ w_out


def moe_swiglu(h, w_router, we_gate, we_up, we_down):
    B, T, D = h.shape
    hf = h.reshape(B * T, D)
    logits = (hf @ w_router).astype(jnp.float32)
    top_v, top_i = lax.top_k(logits, TOP_K)
    gates = jax.nn.softmax(top_v, axis=-1)
    comb = jnp.sum(jax.nn.one_hot(top_i, N_EXPERTS, dtype=jnp.float32) * gates[..., None], axis=1)
    comb = comb.astype(h.dtype)
    y = jnp.zeros_like(hf)
    for e in range(N_EXPERTS):
        y = y + comb[:, e:e + 1] * swiglu(hf, we_gate[e], we_up[e], we_down[e])
    return y.reshape(B, T, D)


def setup_inputs(seed: int = 0) -> dict:
    key = jax.random.key(seed)
    ks = jax.random.split(key, 24)
    f32 = jnp.float32
    nrm = lambda k, s, sc: jax.random.normal(k, s, f32) * sc
    D = D_MODEL
    return {
        "x": jax.random.normal(ks[0], (BATCH, SEQ, D), f32),
        "ln1_w": 1.0 + nrm(ks[1], (DEPTH, D), 0.02),
        "w_in": nrm(ks[2], (DEPTH, D, IN_COLS), D ** -0.5),
        "w_gate_up": nrm(ks[3], (DEPTH, GATE_RANK, GLA_KEY), GATE_RANK ** -0.5),
        "b_gate": nrm(ks[4], (DEPTH, GLA_KEY), 0.1),
        "gla_norm_w": 1.0 + nrm(ks[5], (DEPTH, GLA_DV), 0.02),
        "conv_w": nrm(ks[6], (DEPTH, CONV_WIDTH, CONV_CH), CONV_WIDTH ** -0.5),
        "conv_b": nrm(ks[7], (DEPTH, CONV_CH), 0.02),
        "cn_w": 1.0 + nrm(ks[8], (DEPTH, CONV_CH), 0.02),
        "cn_b": nrm(ks[9], (DEPTH, CONV_CH), 0.02),
        "w_out": nrm(ks[10], (DEPTH, MIX_WIDTH, D), MIX_WIDTH ** -0.5),
        "ln2_w": 1.0 + nrm(ks[11], (DEPTH, D), 0.02),
        "wd_gate": nrm(ks[12], (N_DENSE, D, D_FF), D ** -0.5),
        "wd_up": nrm(ks[13], (N_DENSE, D, D_FF), D ** -0.5),
        "wd_down": nrm(ks[14], (N_DENSE, D_FF, D), D_FF ** -0.5),
        "w_router": nrm(ks[15], (N_MOE, D, N_EXPERTS), D ** -0.5),
        "we_gate": nrm(ks[16], (N_MOE, N_EXPERTS, D, D_FF), D ** -0.5),
        "we_up": nrm(ks[17], (N_MOE, N_EXPERTS, D, D_FF), D ** -0.5),
        "we_down": nrm(ks[18], (N_MOE, N_EXPERTS, D_FF, D), D_FF ** -0.5),
        "final_norm_w": 1.0 + nrm(ks[19], (D,), 0.02),
    }


def reference(x, ln1_w, w_in, w_gate_up, b_gate, gla_norm_w, conv_w, conv_b, cn_w, cn_b, w_out,
              ln2_w, wd_gate, wd_up, wd_down, w_router, we_gate, we_up, we_down, final_norm_w):
    for l in range(DEPTH):
        h = rmsnorm(x, ln1_w[l])
        x = x + token_mixer(h, w_in[l], w_gate_up[l], b_gate[l], gla_norm_w[l], conv_w[l], conv_b[l],
                            cn_w[l], cn_b[l], w_out[l])
        h = rmsnorm(x, ln2_w[l])
        if l % 2 == 0:
            i = l // 2
            x = x + swiglu(h, wd_gate[i], wd_up[i], wd_down[i])
        else:
            i = l // 2
            x = x + moe_swiglu(h, w_router[i], we_gate[i], we_up[i], we_down[i])
    return rmsnorm(x, final_norm_w)
```

```python
import functools
import math

import jax
import jax.numpy as jnp
from jax import lax
from jax.experimental import pallas as pl
from jax.experimental.pallas import tpu as pltpu

D_MODEL = 1024
GLA_HEADS = 4
GLA_DK = 64
GLA_DV = 128
GLA_KEY = GLA_HEADS * GLA_DK
GLA_VAL = GLA_HEADS * GLA_DV
GATE_RANK = 16
GATE_NORMALIZER = 16.0
CONV_CH = 512
CONV_WIDTH = 31
D_FF = 2816
N_EXPERTS = 8
EPS = 1e-6

LANES = 128
CHUNK = 64
TIME_TILE = 512
ROW_TILE = 512
FF_CHUNK = 256
CONV_HALO = 32
GATE_PAD = LANES
IN_COLS_P = 2 * GLA_KEY + 2 * GLA_VAL + 2 * CONV_CH + GATE_PAD
VMEM_LIMIT = 56 * 1024 * 1024

_OQ, _OK, _OV, _OG = 0, GLA_KEY, 2 * GLA_KEY, 2 * GLA_KEY + GLA_VAL
_OCA = _OG + GLA_VAL
_OCB = _OCA + CONV_CH
_OGR = _OCB + CONV_CH

_LEVELS = (32, 16, 8, 4, 2, 1)


def _rms(x, w):
    return x * lax.rsqrt(jnp.mean(x * x, axis=-1, keepdims=True) + EPS) * w


def _bf(x):
    return x.astype(jnp.bfloat16)


def _dot(a, b):
    return jnp.dot(a, b, preferred_element_type=jnp.float32)


def _boundary(b, hs, rows):
    n, c = b.shape
    blk = 2 * hs
    if blk >= 8:
        b3 = b.reshape(n // blk, blk, c)
        return jnp.broadcast_to(b3[:, hs - 1:hs, :], (n // blk, blk, c)).reshape(n, c)
    y = pltpu.roll(b, n - (hs - 1), 0) if hs > 1 else b
    s = 1
    while s < blk:
        y = jnp.where((rows & s) != 0, pltpu.roll(y, s, 0), y)
        s *= 2
    return y


def _mixer_kernel(x_ref, ln_ref, win_ref, wgu_ref, bg_ref, gnw_ref, cw_ref, cb_ref, cnw_ref, cnb_ref,
                  wout_ref, o_ref, s_ref, ubuf_ref, y_ref):
    tt = x_ref.shape[1]
    nch = tt // CHUNK
    t = pl.program_id(1)

    @pl.when(t == 0)
    def _():
        s_ref[...] = jnp.zeros_like(s_ref)
        ubuf_ref[0:CONV_HALO, :] = jnp.zeros((CONV_HALO, CONV_CH), jnp.float32)

    x = x_ref[0]
    h = _bf(_rms(x, ln_ref[...]))
    z = _dot(h, win_ref[...])

    q = z[:, _OQ:_OQ + GLA_KEY] * (GLA_DK ** -0.5)
    k = z[:, _OK:_OK + GLA_KEY]
    v = z[:, _OV:_OV + GLA_VAL]
    g = z[:, _OG:_OG + GLA_VAL]
    ca = z[:, _OCA:_OCA + CONV_CH]
    cb = z[:, _OCB:_OCB + CONV_CH]
    gr = z[:, _OGR:_OGR + GATE_PAD]

    ubuf_ref[CONV_HALO:CONV_HALO + tt, :] = ca * jax.nn.sigmoid(cb)
    off0 = CONV_HALO - (CONV_WIDTH - 1)
    for r0 in range(0, tt, CHUNK):
        for c0 in range(0, CONV_CH, LANES):
            acc = jnp.broadcast_to(cb_ref[:, c0:c0 + LANES], (CHUNK, LANES))
            for j in range(CONV_WIDTH):
                acc = acc + cw_ref[j:j + 1, c0:c0 + LANES] * ubuf_ref[r0 + off0 + j:r0 + off0 + j + CHUNK,
                                                                       c0:c0 + LANES]
            y_ref[r0:r0 + CHUNK, c0:c0 + LANES] = acc
    ubuf_ref[0:CONV_HALO, :] = ubuf_ref[tt:tt + CONV_HALO, :]
    yc = y_ref[...]
    mu = jnp.mean(yc, axis=-1, keepdims=True)
    yd = yc - mu
    var = jnp.mean(yd * yd, axis=-1, keepdims=True)
    u = yd * lax.rsqrt(var + EPS) * cnw_ref[...] + cnb_ref[...]
    u = u * jax.nn.sigmoid(u)

    logit = _dot(_bf(gr), wgu_ref[...]) + bg_ref[...]
    la = jax.nn.log_sigmoid(logit) * (1.0 / GATE_NORMALIZER)
    rows = lax.broadcasted_iota(jnp.int32, (tt, GLA_KEY), 0)
    rc = rows & (CHUNK - 1)
    b = la
    s = 1
    while s < CHUNK:
        b = b + jnp.where(rc >= s, pltpu.roll(b, s, 0), 0.0)
        s *= 2
    b3 = b.reshape(nch, CHUNK, GLA_KEY)
    blast3 = b3[:, CHUNK - 1:CHUNK, :]
    blast = jnp.broadcast_to(blast3, (nch, CHUNK, GLA_KEY)).reshape(tt, GLA_KEY)
    qe = _bf(q * jnp.exp(b))
    kl = _bf(k * jnp.exp(blast - b))
    vb = _bf(v)
    dl = jnp.exp(blast3.reshape(nch, GLA_KEY))
    dl_t = jnp.transpose(jnp.concatenate([dl, jnp.zeros((LANES - nch, GLA_KEY), jnp.float32)], axis=0))

    qh, kh = [_bf(q)], [_bf(k)]
    for hs in _LEVELS:
        e = jnp.exp(-jnp.abs(b - _boundary(b, hs, rows)))
        qh.append(_bf(q * e))
        kh.append(_bf(k * e))

    ii = lax.broadcasted_iota(jnp.int32, (CHUNK, GLA_HEADS * CHUNK), 0)
    jj = lax.broadcasted_iota(jnp.int32, (CHUNK, GLA_HEADS * CHUNK), 1) & (CHUNK - 1)
    masks = [ii == jj]
    for hs in _LEVELS:
        blk = 2 * hs
        masks.append(((ii // blk) == (jj // blk)) & ((ii & (blk - 1)) >= hs) & ((jj & (blk - 1)) < hs))
    rk = lax.broadcasted_iota(jnp.int32, (GLA_HEADS * CHUNK, GLA_KEY), 0) // CHUNK
    ck = lax.broadcasted_iota(jnp.int32, (GLA_HEADS * CHUNK, GLA_KEY), 1) // GLA_DK
    bd_k = (rk == ck).astype(jnp.bfloat16)
    rv = lax.broadcasted_iota(jnp.int32, (GLA_HEADS * CHUNK, GLA_VAL), 0) // CHUNK
    cv = lax.broadcasted_iota(jnp.int32, (GLA_HEADS * CHUNK, GLA_VAL), 1) // GLA_DV
    bd_v = rv == cv
    bd_vb = bd_v.astype(jnp.bfloat16)

    st = s_ref[...]
    o_parts = []
    for c in range(nch):
        sl = slice(c * CHUNK, (c + 1) * CHUNK)
        att = jnp.zeros((CHUNK, GLA_HEADS * CHUNK), jnp.float32)
        for lvl in range(len(masks)):
            kbd = jnp.concatenate([kh[lvl][sl]] * GLA_HEADS, axis=0) * bd_k
            sc = lax.dot_general(qh[lvl][sl], kbd, (((1,), (1,)), ((), ())),
                                 preferred_element_type=jnp.float32)
            att = jnp.where(masks[lvl], sc, att)
        vbd = jnp.concatenate([vb[sl]] * GLA_HEADS, axis=0) * bd_vb
        o_parts.append(_dot(_bf(att), vbd) + _dot(qe[sl], _bf(st)))
        upd = lax.dot_general(kl[sl], vb[sl], (((0,), (0,)), ((), ())), preferred_element_type=jnp.float32)
        st = st * dl_t[:, c:c + 1] + jnp.where(bd_v, upd, 0.0)
    s_ref[...] = st
    o = jnp.concatenate(o_parts, axis=0)

    gnw = gnw_ref[...]
    heads = []
    for hd in range(GLA_HEADS):
        oh = o[:, hd * GLA_DV:(hd + 1) * GLA_DV]
        heads.append(_rms(oh, gnw))
    o = jnp.concatenate(heads, axis=-1) * (g * jax.nn.sigmoid(g))

    mix = _bf(jnp.concatenate([o, u], axis=-1))
    o_ref[0] = x + _dot(mix, wout_ref[...])


def _mixer_call(x, ln_w, w_in_p, wgu_p, b_gate, gnw, conv_w, conv_b, cn_w, cn_b, w_out_b):
    bsz, seq, d = x.shape
    tt = min(TIME_TILE, seq)
    assert seq % tt == 0 and tt % CHUNK == 0
    const = lambda *shape: pl.BlockSpec(shape, lambda b, t: (0,) * len(shape))
    return pl.pallas_call(
        _mixer_kernel,
        out_shape=jax.ShapeDtypeStruct(x.shape, x.dtype),
        grid=(bsz, seq // tt),
        in_specs=[
            pl.BlockSpec((1, tt, d), lambda b, t: (b, t, 0)),
            const(1, d), const(d, IN_COLS_P), const(GATE_PAD, GLA_KEY), const(1, GLA_KEY), const(1, GLA_DV),
            const(CONV_HALO, CONV_CH), const(1, CONV_CH), const(1, CONV_CH), const(1, CONV_CH),
            const(GLA_VAL + CONV_CH, d),
        ],
        out_specs=pl.BlockSpec((1, tt, d), lambda b, t: (b, t, 0)),
        scratch_shapes=[
            pltpu.VMEM((GLA_HEADS * GLA_DK, GLA_VAL), jnp.float32),
            pltpu.VMEM((CONV_HALO + tt, CONV_CH), jnp.float32),
            pltpu.VMEM((tt, CONV_CH), jnp.float32),
        ],
        compiler_params=pltpu.CompilerParams(dimension_semantics=("arbitrary", "arbitrary"),
                                             vmem_limit_bytes=VMEM_LIMIT),
        name="mixer",
    )(x, ln_w, w_in_p, wgu_p, b_gate, gnw, conv_w, conv_b, cn_w, cn_b, w_out_b)


def _ffn_kernel(x_ref, ln_ref, comb_ref, fin_ref, wg_ref, wu_ref, wd_ref, o_ref, h_ref, acc_ref,
                *, routed, final_norm):
    e = pl.program_id(1)
    ne = pl.num_programs(1)

    @pl.when(e == 0)
    def _():
        x = x_ref[...]
        h_ref[...] = _bf(_rms(x, ln_ref[...]))
        acc_ref[...] = x

    h = h_ref[...]
    if routed:
        lane = lax.broadcasted_iota(jnp.int32, comb_ref.shape, 1)
        scale = jnp.sum(jnp.where(lane == e, comb_ref[...], 0.0), axis=-1, keepdims=True)
    y = None
    for f0 in range(0, D_FF, FF_CHUNK):
        gt = _dot(h, wg_ref[0, :, f0:f0 + FF_CHUNK])
        up = _dot(h, wu_ref[0, :, f0:f0 + FF_CHUNK])
        a = _bf(gt * jax.nn.sigmoid(gt) * up)
        part = _dot(a, wd_ref[0, f0:f0 + FF_CHUNK, :])
        y = part if y is None else y + part
    if routed:
        y = y * scale
    acc_ref[...] += y

    @pl.when(e == ne - 1)
    def _():
        out = acc_ref[...]
        if final_norm:
            out = _rms(out, fin_ref[...])
        o_ref[...] = out


def _ffn_call(x2, ln_w, comb, fin_w, wg, wu, wd, *, routed, final_norm):
    n, d = x2.shape
    ne = wg.shape[0]
    tm = min(ROW_TILE, n)
    assert n % tm == 0
    kern = functools.partial(_ffn_kernel, routed=routed, final_norm=final_norm)
    return pl.pallas_call(
        kern,
        out_shape=jax.ShapeDtypeStruct(x2.shape, x2.dtype),
        grid=(n // tm, ne),
        in_specs=[
            pl.BlockSpec((tm, d), lambda i, e: (i, 0)),
            pl.BlockSpec((1, d), lambda i, e: (0, 0)),
            pl.BlockSpec((tm, LANES), lambda i, e: (i, 0)),
            pl.BlockSpec((1, d), lambda i, e: (0, 0)),
            pl.BlockSpec((1, d, D_FF), lambda i, e: (e, 0, 0)),
            pl.BlockSpec((1, d, D_FF), lambda i, e: (e, 0, 0)),
            pl.BlockSpec((1, D_FF, d), lambda i, e: (e, 0, 0)),
        ],
        out_specs=pl.BlockSpec((tm, d), lambda i, e: (i, 0)),
        scratch_shapes=[pltpu.VMEM((tm, d), jnp.bfloat16), pltpu.VMEM((tm, d), jnp.float32)],
        compiler_params=pltpu.CompilerParams(dimension_semantics=("arbitrary", "arbitrary"),
                                             vmem_limit_bytes=VMEM_LIMIT),
        name="ffn_routed" if routed else "ffn_dense",
    )(x2, ln_w, comb, fin_w, wg, wu, wd)


def _router_kernel(x_ref, ln_ref, wr_ref, comb_ref):
    h = _rms(x_ref[...], ln_ref[...])
    logits = jnp.dot(h, wr_ref[...], preferred_element_type=jnp.float32, precision=lax.Precision.HIGHEST)
    lane = lax.broadcasted_iota(jnp.int32, logits.shape, 1)
    neg = jnp.float32(-jnp.inf)
    logits = jnp.where(lane < N_EXPERTS, logits, neg)
    m1 = jnp.max(logits, axis=-1, keepdims=True)
    i1 = jnp.min(jnp.where(logits == m1, lane, LANES), axis=-1, keepdims=True)
    rest = jnp.where(lane == i1, neg, logits)
    m2 = jnp.max(rest, axis=-1, keepdims=True)
    i2 = jnp.min(jnp.where(rest == m2, lane, LANES), axis=-1, keepdims=True)
    e2 = jnp.exp(m2 - m1)
    den = 1.0 + e2
    comb_ref[...] = jnp.where(lane == i1, 1.0 / den, 0.0) + jnp.where(lane == i2, e2 / den, 0.0)


def _router_call(x2, ln_w, wr_p):
    n, d = x2.shape
    tm = min(ROW_TILE, n)
    return pl.pallas_call(
        _router_kernel,
        out_shape=jax.ShapeDtypeStruct((n, LANES), jnp.float32),
        grid=(n // tm,),
        in_specs=[
            pl.BlockSpec((tm, d), lambda i: (i, 0)),
            pl.BlockSpec((1, d), lambda i: (0, 0)),
            pl.BlockSpec((d, LANES), lambda i: (0, 0)),
        ],
        out_specs=pl.BlockSpec((tm, LANES), lambda i: (i, 0)),
        compiler_params=pltpu.CompilerParams(dimension_semantics=("arbitrary",), vmem_limit_bytes=VMEM_LIMIT),
        name="router",
    )(x2, ln_w, wr_p)


def _prep_w_in(w):
    d = w.shape[0]
    o_gr = 2 * GLA_KEY + 2 * GLA_VAL
    parts = [w[:, :o_gr], w[:, o_gr + GATE_RANK:], w[:, o_gr:o_gr + GATE_RANK],
             jnp.zeros((d, GATE_PAD - GATE_RANK), w.dtype)]
    return _bf(jnp.concatenate(parts, axis=1))


def kernel(x, ln1_w, w_in, w_gate_up, b_gate, gla_norm_w, conv_w, conv_b, cn_w, cn_b, w_out, ln2_w, wd_gate,
           wd_up, wd_down, w_router, we_gate, we_up, we_down, final_norm_w):
    bsz, seq, d = x.shape
    depth = ln1_w.shape[0]
    n = bsz * seq
    row = lambda a: a.reshape(1, -1)
    zeros_comb = jnp.zeros((n, LANES), jnp.float32)
    for l in range(depth):
        wgu_p = _bf(jnp.concatenate(
            [w_gate_up[l], jnp.zeros((GATE_PAD - GATE_RANK, GLA_KEY), w_gate_up.dtype)], axis=0))
        cw_p = jnp.concatenate([conv_w[l], jnp.zeros((CONV_HALO - CONV_WIDTH, CONV_CH), conv_w.dtype)], axis=0)
        x = _mixer_call(x, row(ln1_w[l]), _prep_w_in(w_in[l]), wgu_p, row(b_gate[l]), row(gla_norm_w[l]),
                        cw_p, row(conv_b[l]), row(cn_w[l]), row(cn_b[l]), _bf(w_out[l]))
        x2 = x.reshape(n, d)
        last = l == depth - 1
        i = l // 2
        if l % 2 == 0:
            x2 = _ffn_call(x2, row(ln2_w[l]), zeros_comb, row(final_norm_w), _bf(wd_gate[i:i + 1]),
                           _bf(wd_up[i:i + 1]), _bf(wd_down[i:i + 1]), routed=False, final_norm=last)
        else:
            wr_p = jnp.concatenate([w_router[i], jnp.zeros((d, LANES - N_EXPERTS), w_router.dtype)], axis=1)
            comb = _router_call(x2, row(ln2_w[l]), wr_p)
            x2 = _ffn_call(x2, row(ln2_w[l]), comb, row(final_norm_w), _bf(we_gate[i]), _bf(we_up[i]),
                           _bf(we_down[i]), routed=True, final_norm=last)
        x = x2.reshape(bsz, seq, d)
    return x
```

```python
import functools

import jax
import jax.numpy as jnp
from jax import lax
from jax.experimental import pallas as pl
from jax.experimental.pallas import tpu as pltpu

D_MODEL = 1024
GLA_HEADS = 4
GLA_DK = 64
GLA_DV = 128
GLA_KEY = GLA_HEADS * GLA_DK
GLA_VAL = GLA_HEADS * GLA_DV
GATE_RANK = 16
GATE_NORMALIZER = 16.0
CONV_CH = 512
CONV_WIDTH = 31
D_FF = 2816
N_EXPERTS = 8
TOP_K = 2
EPS = 1e-6

LANES = 128
CHUNK = 64
TIME_TILE = 512
ROW_TILE = 512
FF_CHUNK = 256
CONV_HALO = 32
GATE_PAD = LANES
IN_COLS_P = 2 * GLA_KEY + 2 * GLA_VAL + 2 * CONV_CH + GATE_PAD
VMEM_LIMIT = 56 * 1024 * 1024
SUBLANES = 8
REC_ROWS = D_MODEL // LANES

_OQ, _OK, _OV, _OG = 0, GLA_KEY, 2 * GLA_KEY, 2 * GLA_KEY + GLA_VAL
_OCA = _OG + GLA_VAL
_OCB = _OCA + CONV_CH
_OGR = _OCB + CONV_CH

_LEVELS = (32, 16, 8, 4, 2, 1)


def _rms(x, w):
    return x * lax.rsqrt(jnp.mean(x * x, axis=-1, keepdims=True) + EPS) * w


def _bf(x):
    return x.astype(jnp.bfloat16)


def _dot(a, b):
    return jnp.dot(a, b, preferred_element_type=jnp.float32)


def _boundary(b, hs, rows):
    n, c = b.shape
    blk = 2 * hs
    if blk >= 8:
        b3 = b.reshape(n // blk, blk, c)
        return jnp.broadcast_to(b3[:, hs - 1:hs, :], (n // blk, blk, c)).reshape(n, c)
    y = pltpu.roll(b, n - (hs - 1), 0) if hs > 1 else b
    s = 1
    while s < blk:
        y = jnp.where((rows & s) != 0, pltpu.roll(y, s, 0), y)
        s *= 2
    return y


def _mixer_kernel(x_ref, ln_ref, win_ref, wgu_ref, bg_ref, gnw_ref, cw_ref, cb_ref, cnw_ref, cnb_ref,
                  wout_ref, o_ref, s_ref, ubuf_ref, ush_ref, y_ref):
    tt = x_ref.shape[1]
    nch = tt // CHUNK
    t = pl.program_id(1)

    @pl.when(t == 0)
    def _():
        s_ref[...] = jnp.zeros_like(s_ref)
        ubuf_ref[0:CONV_HALO, :] = jnp.zeros((CONV_HALO, CONV_CH), jnp.float32)

    x = x_ref[0]
    h = _bf(_rms(x, ln_ref[...]))
    z = _dot(h, win_ref[...])

    q = z[:, _OQ:_OQ + GLA_KEY] * (GLA_DK ** -0.5)
    k = z[:, _OK:_OK + GLA_KEY]
    v = z[:, _OV:_OV + GLA_VAL]
    g = z[:, _OG:_OG + GLA_VAL]
    ca = z[:, _OCA:_OCA + CONV_CH]
    cb = z[:, _OCB:_OCB + CONV_CH]
    gr = z[:, _OGR:_OGR + GATE_PAD]

    ubuf_ref[CONV_HALO:CONV_HALO + tt, :] = ca * jax.nn.sigmoid(cb)
    off0 = CONV_HALO - (CONV_WIDTH - 1)
    for r in range(1, SUBLANES):
        ush_ref[r - 1] = ubuf_ref[r:r + CONV_HALO + tt - SUBLANES, :]
    for r0 in range(0, tt, CHUNK):
        for c0 in range(0, CONV_CH, LANES):
            acc = jnp.broadcast_to(cb_ref[:, c0:c0 + LANES], (CHUNK, LANES))
            for j in range(CONV_WIDTH):
                r = (off0 + j) % SUBLANES
                a0 = r0 + off0 + j - r
                src = ubuf_ref if r == 0 else ush_ref.at[r - 1]
                acc = acc + cw_ref[j:j + 1, c0:c0 + LANES] * src[a0:a0 + CHUNK, c0:c0 + LANES]
            y_ref[r0:r0 + CHUNK, c0:c0 + LANES] = acc
    ubuf_ref[0:CONV_HALO, :] = ubuf_ref[tt:tt + CONV_HALO, :]
    yc = y_ref[...]
    mu = jnp.mean(yc, axis=-1, keepdims=True)
    yd = yc - mu
    var = jnp.mean(yd * yd, axis=-1, keepdims=True)
    u = yd * lax.rsqrt(var + EPS) * cnw_ref[...] + cnb_ref[...]
    u = u * jax.nn.sigmoid(u)

    logit = _dot(_bf(gr), wgu_ref[...]) + bg_ref[...]
    la = jax.nn.log_sigmoid(logit) * (1.0 / GATE_NORMALIZER)
    rows = lax.broadcasted_iota(jnp.int32, (tt, GLA_KEY), 0)
    rc = rows & (CHUNK - 1)
    b = la
    s = 1
    while s < CHUNK:
        b = b + jnp.where(rc >= s, pltpu.roll(b, s, 0), 0.0)
        s *= 2
    b3 = b.reshape(nch, CHUNK, GLA_KEY)
    blast3 = b3[:, CHUNK - 1:CHUNK, :]
    blast = jnp.broadcast_to(blast3, (nch, CHUNK, GLA_KEY)).reshape(tt, GLA_KEY)
    qe = _bf(q * jnp.exp(b))
    kl = _bf(k * jnp.exp(blast - b))
    vb = _bf(v)
    dl = jnp.exp(blast3.reshape(nch, GLA_KEY))
    dl_t = jnp.transpose(jnp.concatenate([dl, jnp.zeros((LANES - nch, GLA_KEY), jnp.float32)], axis=0))

    qh, kh = [_bf(q)], [_bf(k)]
    for hs in _LEVELS:
        e = jnp.exp(-jnp.abs(b - _boundary(b, hs, rows)))
        qh.append(_bf(q * e))
        kh.append(_bf(k * e))

    ii = lax.broadcasted_iota(jnp.int32, (CHUNK, GLA_HEADS * CHUNK), 0)
    jj = lax.broadcasted_iota(jnp.int32, (CHUNK, GLA_HEADS * CHUNK), 1) & (CHUNK - 1)
    masks = [ii == jj]
    for hs in _LEVELS:
        blk = 2 * hs
        masks.append(((ii // blk) == (jj // blk)) & ((ii & (blk - 1)) >= hs) & ((jj & (blk - 1)) < hs))
    rk = lax.broadcasted_iota(jnp.int32, (GLA_HEADS * CHUNK, GLA_KEY), 0) // CHUNK
    ck = lax.broadcasted_iota(jnp.int32, (GLA_HEADS * CHUNK, GLA_KEY), 1) // GLA_DK
    bd_k = (rk == ck).astype(jnp.bfloat16)
    rv = lax.broadcasted_iota(jnp.int32, (GLA_HEADS * CHUNK, GLA_VAL), 0) // CHUNK
    cv = lax.broadcasted_iota(jnp.int32, (GLA_HEADS * CHUNK, GLA_VAL), 1) // GLA_DV
    bd_v = rv == cv
    bd_vb = bd_v.astype(jnp.bfloat16)

    st = s_ref[...]
    o_parts = []
    for c in range(nch):
        sl = slice(c * CHUNK, (c + 1) * CHUNK)
        att = jnp.zeros((CHUNK, GLA_HEADS * CHUNK), jnp.float32)
        for lvl in range(len(masks)):
            kbd = jnp.concatenate([kh[lvl][sl]] * GLA_HEADS, axis=0) * bd_k
            sc = lax.dot_general(qh[lvl][sl], kbd, (((1,), (1,)), ((), ())),
                                 preferred_element_type=jnp.float32)
            att = jnp.where(masks[lvl], sc, att)
        vbd = jnp.concatenate([vb[sl]] * GLA_HEADS, axis=0) * bd_vb
        o_parts.append(_dot(_bf(att), vbd) + _dot(qe[sl], _bf(st)))
        upd = lax.dot_general(kl[sl], vb[sl], (((0,), (0,)), ((), ())), preferred_element_type=jnp.float32)
        st = st * dl_t[:, c:c + 1] + jnp.where(bd_v, upd, 0.0)
    s_ref[...] = st
    o = jnp.concatenate(o_parts, axis=0)

    gnw = gnw_ref[...]
    heads = []
    for hd in range(GLA_HEADS):
        oh = o[:, hd * GLA_DV:(hd + 1) * GLA_DV]
        heads.append(_rms(oh, gnw))
    o = jnp.concatenate(heads, axis=-1) * (g * jax.nn.sigmoid(g))

    mix = _bf(jnp.concatenate([o, u], axis=-1))
    o_ref[0] = x + _dot(mix, wout_ref[...])


def _mixer_call(x, ln_w, w_in_p, wgu_p, b_gate, gnw, conv_w, conv_b, cn_w, cn_b, w_out_b):
    bsz, seq, d = x.shape
    tt = min(TIME_TILE, seq)
    assert seq % tt == 0 and tt % CHUNK == 0
    const = lambda *shape: pl.BlockSpec(shape, lambda b, t: (0,) * len(shape))
    return pl.pallas_call(
        _mixer_kernel,
        out_shape=jax.ShapeDtypeStruct(x.shape, x.dtype),
        grid=(bsz, seq // tt),
        in_specs=[
            pl.BlockSpec((1, tt, d), lambda b, t: (b, t, 0)),
            const(1, d), const(d, IN_COLS_P), const(GATE_PAD, GLA_KEY), const(1, GLA_KEY), const(1, GLA_DV),
            const(CONV_HALO, CONV_CH), const(1, CONV_CH), const(1, CONV_CH), const(1, CONV_CH),
            const(GLA_VAL + CONV_CH, d),
        ],
        out_specs=pl.BlockSpec((1, tt, d), lambda b, t: (b, t, 0)),
        scratch_shapes=[
            pltpu.VMEM((GLA_HEADS * GLA_DK, GLA_VAL), jnp.float32),
            pltpu.VMEM((CONV_HALO + tt, CONV_CH), jnp.float32),
            pltpu.VMEM((SUBLANES - 1, CONV_HALO + tt - SUBLANES, CONV_CH), jnp.float32),
            pltpu.VMEM((tt, CONV_CH), jnp.float32),
        ],
        compiler_params=pltpu.CompilerParams(dimension_semantics=("arbitrary", "arbitrary"),
                                             vmem_limit_bytes=VMEM_LIMIT),
        name="mixer",
    )(x, ln_w, w_in_p, wgu_p, b_gate, gnw, conv_w, conv_b, cn_w, cn_b, w_out_b)


def _swiglu(h, wg_ref, wu_ref, wd_ref):
    y = None
    for f0 in range(0, D_FF, FF_CHUNK):
        gt = _dot(h, wg_ref[0, :, f0:f0 + FF_CHUNK])
        up = _dot(h, wu_ref[0, :, f0:f0 + FF_CHUNK])
        a = _bf(gt * jax.nn.sigmoid(gt) * up)
        part = _dot(a, wd_ref[0, f0:f0 + FF_CHUNK, :])
        y = part if y is None else y + part
    return y


def _ffn_dense_kernel(x_ref, ln_ref, fin_ref, wg_ref, wu_ref, wd_ref, o_ref, *, final_norm):
    x = x_ref[...]
    out = x + _swiglu(_bf(_rms(x, ln_ref[...])), wg_ref, wu_ref, wd_ref)
    if final_norm:
        out = _rms(out, fin_ref[...])
    o_ref[...] = out


def _ffn_dense_call(x2, ln_w, fin_w, wg, wu, wd, *, final_norm):
    n, d = x2.shape
    tm = min(ROW_TILE, n)
    assert n % tm == 0
    return pl.pallas_call(
        functools.partial(_ffn_dense_kernel, final_norm=final_norm),
        out_shape=jax.ShapeDtypeStruct(x2.shape, x2.dtype),
        grid=(n // tm,),
        in_specs=[
            pl.BlockSpec((tm, d), lambda i: (i, 0)),
            pl.BlockSpec((1, d), lambda i: (0, 0)),
            pl.BlockSpec((1, d), lambda i: (0, 0)),
            pl.BlockSpec((1, d, D_FF), lambda i: (0, 0, 0)),
            pl.BlockSpec((1, d, D_FF), lambda i: (0, 0, 0)),
            pl.BlockSpec((1, D_FF, d), lambda i: (0, 0, 0)),
        ],
        out_specs=pl.BlockSpec((tm, d), lambda i: (i, 0)),
        compiler_params=pltpu.CompilerParams(dimension_semantics=("arbitrary",), vmem_limit_bytes=VMEM_LIMIT),
        name="ffn_dense",
    )(x2, ln_w, fin_w, wg, wu, wd)


def _store_records(rec_ref, vals):
    m = vals.shape[0]
    for s in range(REC_ROWS):
        rec_ref[pl.ds(s, m, stride=REC_ROWS), :] = vals[:, s * LANES:(s + 1) * LANES]


def _load_records(rec_ref, m):
    return [rec_ref[pl.ds(s, m, stride=REC_ROWS), :] for s in range(REC_ROWS)]


def _copy_records(src_ref, src_off, dst_ref, dst_off, n, sem, max_rows, wait=False):
    bit = max_rows.bit_length() - 1
    while bit >= 0:
        size = (1 << bit) * REC_ROWS
        done = lax.shift_left(lax.shift_right_logical(n, bit + 1), bit + 1)
        src0 = 0 if src_off is None else pl.multiple_of((src_off + done) * REC_ROWS, REC_ROWS)
        dst0 = pl.multiple_of((dst_off + done) * REC_ROWS, REC_ROWS)

        @pl.when((lax.shift_right_logical(n, bit) & 1) == 1)
        def _(size=size, src0=src0, dst0=dst0):
            cp = pltpu.make_async_copy(src_ref.at[pl.ds(src0, size)], dst_ref.at[pl.ds(dst0, size)], sem)
            cp.wait() if wait else cp.start()
        bit -= 1


def _one_hot_rows(pos0, pos1, n_rows):
    r = lax.broadcasted_iota(jnp.int32, (n_rows, pos0.shape[1]), 0)
    return jnp.where((r == pos0) | (r == pos1), 1.0, 0.0).astype(jnp.bfloat16)


def _route_kernel(x_ref, ln_ref, wr_ref, pos_ref, gate_ref, cnt_ref):
    ts = x_ref.shape[0]
    t = pl.program_id(0)
    h = _rms(x_ref[...], ln_ref[...])
    logits = jnp.dot(h, wr_ref[...], preferred_element_type=jnp.float32, precision=lax.Precision.HIGHEST)
    lt = jnp.transpose(logits)[0:N_EXPERTS, :]
    row = lax.broadcasted_iota(jnp.int32, lt.shape, 0)
    neg = jnp.float32(-jnp.inf)
    m1 = jnp.max(lt, axis=0, keepdims=True)
    i1 = jnp.min(jnp.where(lt == m1, row, N_EXPERTS), axis=0, keepdims=True)
    rest = jnp.where(row == i1, neg, lt)
    m2 = jnp.max(rest, axis=0, keepdims=True)
    i2 = jnp.min(jnp.where(rest == m2, row, N_EXPERTS), axis=0, keepdims=True)
    e2 = jnp.exp(m2 - m1)
    den = 1.0 + e2
    g1 = 1.0 / den
    g2 = e2 / den
    sel1 = row == i1
    sel2 = row == i2
    oh = jnp.where(sel1 | sel2, 1.0, 0.0)

    sp = lax.broadcasted_iota(jnp.int32, (ts, ts), 0)
    sc = lax.broadcasted_iota(jnp.int32, (ts, ts), 1)
    upper = jnp.where(sp < sc, 1.0, 0.0).astype(jnp.bfloat16)
    rank = _dot(_bf(oh), upper)
    rk1 = jnp.sum(jnp.where(sel1, rank, 0.0), axis=0, keepdims=True)
    rk2 = jnp.sum(jnp.where(sel2, rank, 0.0), axis=0, keepdims=True)

    cnts, offs = [], []
    off = jnp.int32(0)
    for e in range(N_EXPERTS):
        c = jnp.sum(oh[e:e + 1, :]).astype(jnp.int32)
        cnts.append(c)
        offs.append(off)
        off = off + c
    off1 = jnp.zeros_like(rk1)
    off2 = jnp.zeros_like(rk2)
    for e in range(N_EXPERTS):
        fe = offs[e].astype(jnp.float32)
        off1 = jnp.where(i1 == e, fe, off1)
        off2 = jnp.where(i2 == e, fe, off2)
    pos1 = (off1 + rk1).astype(jnp.int32)
    pos2 = (off2 + rk2).astype(jnp.int32)
    pos_ref[0] = jnp.concatenate([pos1, pos2, jnp.zeros((SUBLANES - TOP_K, ts), jnp.int32)], axis=0)
    gate_ref[0] = jnp.concatenate([g1, g2, jnp.zeros((SUBLANES - TOP_K, ts), jnp.float32)], axis=0)
    for e in range(N_EXPERTS):
        cnt_ref[t * N_EXPERTS + e] = cnts[e]


def _route_call(x2, ln_w, wr_p):
    n, d = x2.shape
    ts = min(ROW_TILE, n)
    nt = n // ts
    return pl.pallas_call(
        _route_kernel,
        out_shape=(
            jax.ShapeDtypeStruct((nt, SUBLANES, ts), jnp.int32),
            jax.ShapeDtypeStruct((nt, SUBLANES, ts), jnp.float32),
            jax.ShapeDtypeStruct((nt * N_EXPERTS,), jnp.int32),
        ),
        grid=(nt,),
        in_specs=[
            pl.BlockSpec((ts, d), lambda i: (i, 0)),
            pl.BlockSpec((1, d), lambda i: (0, 0)),
            pl.BlockSpec((d, LANES), lambda i: (0, 0)),
        ],
        out_specs=(
            pl.BlockSpec((1, SUBLANES, ts), lambda i: (i, 0, 0)),
            pl.BlockSpec((1, SUBLANES, ts), lambda i: (i, 0, 0)),
            pl.BlockSpec(memory_space=pltpu.SMEM),
        ),
        compiler_params=pltpu.CompilerParams(dimension_semantics=("arbitrary",), vmem_limit_bytes=VMEM_LIMIT),
        name="route",
    )(x2, ln_w, wr_p)


def _dispatch_kernel(seg_ref, cnt_ref, zstart_ref, zlen_ref, nval_ref, x_ref, ln_ref, pos_ref, xs_hbm,
                     stage_ref, zero_ref, sem, zsem, *, max_tiles):
    ts = x_ref.shape[0]
    n_sorted = TOP_K * ts
    t = pl.program_id(0)
    nt = pl.num_programs(0)
    h = _bf(_rms(x_ref[...], ln_ref[...]))
    pos = pos_ref[0]
    perm = _one_hot_rows(pos[0:1, :], pos[1:2, :], n_sorted)
    sorted_h = _dot(perm, h)

    @pl.when(t > 0)
    def _():
        pltpu.make_async_copy(stage_ref, xs_hbm.at[pl.ds(0, n_sorted * REC_ROWS)], sem).wait()

    _store_records(stage_ref, sorted_h)

    off = jnp.int32(0)
    for e in range(N_EXPERTS):
        c = cnt_ref[t * N_EXPERTS + e]
        _copy_records(stage_ref, off, xs_hbm, seg_ref[t * N_EXPERTS + e], c, sem, ts)
        off = off + c

    @pl.when(t == nt - 1)
    def _():
        pltpu.make_async_copy(stage_ref, xs_hbm.at[pl.ds(0, n_sorted * REC_ROWS)], sem).wait()
        tm = zero_ref.shape[0] // REC_ROWS
        zero_ref[...] = jnp.zeros_like(zero_ref)

        def tail_fill(k):
            return pltpu.make_async_copy(
                zero_ref, xs_hbm.at[pl.ds(pl.multiple_of((nval_ref[0] + k) * tm * REC_ROWS, REC_ROWS),
                                          tm * REC_ROWS)], zsem)

        for wait in (False, True):
            for e in range(N_EXPERTS):
                _copy_records(zero_ref, None, xs_hbm, zstart_ref[e], zlen_ref[e], zsem, tm, wait=wait)
            for k in range(N_EXPERTS):
                @pl.when(nval_ref[0] + k < max_tiles)
                def _(k=k, wait=wait):
                    tail_fill(k).wait() if wait else tail_fill(k).start()


def _dispatch_call(seg, cnt, zstart, zlen, n_valid, x2, ln_w, pos, max_tiles):
    n, d = x2.shape
    ts = min(ROW_TILE, n)
    nt = n // ts
    return pl.pallas_call(
        functools.partial(_dispatch_kernel, max_tiles=max_tiles),
        out_shape=jax.ShapeDtypeStruct((max_tiles * ROW_TILE * REC_ROWS, LANES), jnp.float32),
        grid_spec=pltpu.PrefetchScalarGridSpec(
            num_scalar_prefetch=5,
            grid=(nt,),
            in_specs=[
                pl.BlockSpec((ts, d), lambda i, *_: (i, 0)),
                pl.BlockSpec((1, d), lambda i, *_: (0, 0)),
                pl.BlockSpec((1, SUBLANES, ts), lambda i, *_: (i, 0, 0)),
            ],
            out_specs=pl.BlockSpec(memory_space=pl.ANY),
            scratch_shapes=[
                pltpu.VMEM((TOP_K * ts * REC_ROWS, LANES), jnp.float32),
                pltpu.VMEM((ROW_TILE * REC_ROWS, LANES), jnp.float32),
                pltpu.SemaphoreType.DMA(()),
                pltpu.SemaphoreType.DMA(()),
            ],
        ),
        compiler_params=pltpu.CompilerParams(dimension_semantics=("arbitrary",), vmem_limit_bytes=VMEM_LIMIT),
        name="dispatch",
    )(seg, cnt, zstart, zlen, n_valid, x2, ln_w, pos)


def _ffn_group_kernel(texp_ref, nval_ref, xs_ref, wg_ref, wu_ref, wd_ref, ys_ref):
    j = pl.program_id(0)

    @pl.when(j >= nval_ref[0])
    def _():
        ys_ref[...] = jnp.zeros_like(ys_ref)

    @pl.when(j < nval_ref[0])
    def _():
        tm = xs_ref.shape[0] // REC_ROWS
        h = jnp.concatenate([_bf(w) for w in _load_records(xs_ref, tm)], axis=1)
        _store_records(ys_ref, _swiglu(h, wg_ref, wu_ref, wd_ref))


def _ffn_group_call(tile_expert, n_valid, xs, wg, wu, wd):
    rows = xs.shape[0]
    tm = ROW_TILE
    n_tiles = tile_expert.shape[0]
    d = wg.shape[1]
    return pl.pallas_call(
        _ffn_group_kernel,
        out_shape=jax.ShapeDtypeStruct((rows, LANES), jnp.float32),
        grid_spec=pltpu.PrefetchScalarGridSpec(
            num_scalar_prefetch=2,
            grid=(n_tiles,),
            in_specs=[
                pl.BlockSpec((tm * REC_ROWS, LANES), lambda j, te, nv: (j, 0)),
                pl.BlockSpec((1, d, D_FF), lambda j, te, nv: (te[j], 0, 0)),
                pl.BlockSpec((1, d, D_FF), lambda j, te, nv: (te[j], 0, 0)),
                pl.BlockSpec((1, D_FF, d), lambda j, te, nv: (te[j], 0, 0)),
            ],
            out_specs=pl.BlockSpec((tm * REC_ROWS, LANES), lambda j, te, nv: (j, 0)),
        ),
        compiler_params=pltpu.CompilerParams(dimension_semantics=("arbitrary",), vmem_limit_bytes=VMEM_LIMIT),
        name="ffn_group",
    )(tile_expert, n_valid, xs, wg, wu, wd)


def _combine_kernel(seg_ref, cnt_ref, x_ref, fin_ref, pos_ref, gate_ref, ys_hbm, o_ref, ybuf_ref, sems,
                    *, final_norm):
    ts = x_ref.shape[0]
    n_sorted = TOP_K * ts
    t = pl.program_id(0)
    nt = pl.num_programs(0)

    def fetch(tile, slot):
        off = jnp.int32(0)
        for e in range(N_EXPERTS):
            c = cnt_ref[tile * N_EXPERTS + e]
            _copy_records(ys_hbm, seg_ref[tile * N_EXPERTS + e], ybuf_ref.at[slot], off, c, sems.at[slot], ts)
            off = off + c

    @pl.when(t == 0)
    def _():
        fetch(0, 0)

    @pl.when(t + 1 < nt)
    def _():
        fetch(t + 1, (t + 1) % 2)

    slot = t % 2
    pltpu.make_async_copy(ys_hbm.at[pl.ds(0, n_sorted * REC_ROWS)], ybuf_ref.at[slot], sems.at[slot]).wait()
    ys = jnp.concatenate([_bf(w) for w in _load_records(ybuf_ref.at[slot], n_sorted)], axis=1)
    pos = pos_ref[0]
    gates = gate_ref[0]
    r = lax.broadcasted_iota(jnp.int32, (n_sorted, ts), 0)
    sel = _bf(jnp.where(r == pos[0:1, :], gates[0:1, :], 0.0) + jnp.where(r == pos[1:2, :], gates[1:2, :], 0.0))
    y = lax.dot_general(sel, ys, (((0,), (0,)), ((), ())), preferred_element_type=jnp.float32)
    out = x_ref[...] + y
    if final_norm:
        out = _rms(out, fin_ref[...])
    o_ref[...] = out


def _combine_call(seg, cnt, x2, fin_w, pos, gates, ys, *, final_norm):
    n, d = x2.shape
    ts = min(ROW_TILE, n)
    nt = n // ts
    return pl.pallas_call(
        functools.partial(_combine_kernel, final_norm=final_norm),
        out_shape=jax.ShapeDtypeStruct(x2.shape, x2.dtype),
        grid_spec=pltpu.PrefetchScalarGridSpec(
            num_scalar_prefetch=2,
            grid=(nt,),
            in_specs=[
                pl.BlockSpec((ts, d), lambda i, sg, ct: (i, 0)),
                pl.BlockSpec((1, d), lambda i, sg, ct: (0, 0)),
                pl.BlockSpec((1, SUBLANES, ts), lambda i, sg, ct: (i, 0, 0)),
                pl.BlockSpec((1, SUBLANES, ts), lambda i, sg, ct: (i, 0, 0)),
                pl.BlockSpec(memory_space=pl.ANY),
            ],
            out_specs=pl.BlockSpec((ts, d), lambda i, sg, ct: (i, 0)),
            scratch_shapes=[
                pltpu.VMEM((2, TOP_K * ts * REC_ROWS, LANES), jnp.float32),
                pltpu.SemaphoreType.DMA((2,)),
            ],
        ),
        compiler_params=pltpu.CompilerParams(dimension_semantics=("arbitrary",), vmem_limit_bytes=VMEM_LIMIT),
        name="combine",
    )(seg, cnt, x2, fin_w, pos, gates, ys)


def _moe(x2, ln_w, fin_w, w_router, wg, wu, wd, *, final_norm):
    n, d = x2.shape
    tm = ROW_TILE
    wr_p = jnp.concatenate([w_router, jnp.zeros((d, LANES - N_EXPERTS), w_router.dtype)], axis=1)
    pos, gates, cnt = _route_call(x2, ln_w, wr_p)

    i32 = jnp.int32
    nt = cnt.shape[0] // N_EXPERTS
    cnt2 = cnt.reshape(nt, N_EXPERTS)
    totals = jnp.sum(cnt2, axis=0)
    tiles_e = (totals + tm - 1) // tm
    ends = jnp.cumsum(tiles_e)
    base = (ends - tiles_e) * tm
    seg = (base[None, :] + jnp.cumsum(cnt2, axis=0) - cnt2).reshape(-1).astype(i32)
    zstart = (base + totals).astype(i32)
    zlen = (tiles_e * tm - totals).astype(i32)
    n_valid = ends[-1].reshape(1).astype(i32)
    max_tiles = (TOP_K * n) // tm + N_EXPERTS
    j = jnp.minimum(jnp.arange(max_tiles, dtype=i32), n_valid - 1)
    tile_expert = jnp.sum((j[:, None] >= ends[None, :]).astype(i32), axis=1).astype(i32)

    xs = _dispatch_call(seg, cnt, zstart, zlen, n_valid, x2, ln_w, pos, max_tiles)
    ys = _ffn_group_call(tile_expert, n_valid, xs, wg, wu, wd)
    return _combine_call(seg, cnt, x2, fin_w, pos, gates, ys, final_norm=final_norm)


def _prep_w_in(w):
    d = w.shape[0]
    o_gr = 2 * GLA_KEY + 2 * GLA_VAL
    parts = [w[:, :o_gr], w[:, o_gr + GATE_RANK:], w[:, o_gr:o_gr + GATE_RANK],
             jnp.zeros((d, GATE_PAD - GATE_RANK), w.dtype)]
    return _bf(jnp.concatenate(parts, axis=1))


def kernel(x, ln1_w, w_in, w_gate_up, b_gate, gla_norm_w, conv_w, conv_b, cn_w, cn_b, w_out, ln2_w, wd_gate,
           wd_up, wd_down, w_router, we_gate, we_up, we_down, final_norm_w):
    bsz, seq, d = x.shape
    depth = ln1_w.shape[0]
    n = bsz * seq
    row = lambda a: a.reshape(1, -1)
    for l in range(depth):
        wgu_p = _bf(jnp.concatenate(
            [w_gate_up[l], jnp.zeros((GATE_PAD - GATE_RANK, GLA_KEY), w_gate_up.dtype)], axis=0))
        cw_p = jnp.concatenate([conv_w[l], jnp.zeros((CONV_HALO - CONV_WIDTH, CONV_CH), conv_w.dtype)], axis=0)
        x = _mixer_call(x, row(ln1_w[l]), _prep_w_in(w_in[l]), wgu_p, row(b_gate[l]), row(gla_norm_w[l]),
                        cw_p, row(conv_b[l]), row(cn_w[l]), row(cn_b[l]), _bf(w_out[l]))
        x2 = x.reshape(n, d)
        last = l == depth - 1
        i = l // 2
        if l % 2 == 0:
            x2 = _ffn_dense_call(x2, row(ln2_w[l]), row(final_norm_w), _bf(wd_gate[i:i + 1]),
                                 _bf(wd_up[i:i + 1]), _bf(wd_down[i:i + 1]), final_norm=last)
        else:
            x2 = _moe(x2, row(ln2_w[l]), row(final_norm_w), w_router[i], _bf(we_gate[i]), _bf(we_up[i]),
                      _bf(we_down[i]), final_norm=last)
        x = x2.reshape(bsz, seq, d)
    return x
```

```python
import functools

import jax
import jax.numpy as jnp
from jax import lax
from jax.experimental import pallas as pl
from jax.experimental.pallas import tpu as pltpu

D_MODEL = 1024
GLA_HEADS = 4
GLA_DK = 64
GLA_DV = 128
GLA_KEY = GLA_HEADS * GLA_DK
GLA_VAL = GLA_HEADS * GLA_DV
GATE_RANK = 16
GATE_NORMALIZER = 16.0
CONV_CH = 512
CONV_WIDTH = 31
D_FF = 2816
N_EXPERTS = 8
TOP_K = 2
EPS = 1e-6

LANES = 128
CHUNK = 64
TIME_TILE = 512
ROW_TILE = 512
FF_CHUNK = 256
CONV_HALO = 32
GATE_PAD = LANES
IN_COLS_P = 2 * GLA_KEY + 2 * GLA_VAL + 2 * CONV_CH + GATE_PAD
VMEM_LIMIT = 56 * 1024 * 1024
SUBLANES = 8
REC_ROWS = D_MODEL // LANES

_OQ, _OK, _OV, _OG = 0, GLA_KEY, 2 * GLA_KEY, 2 * GLA_KEY + GLA_VAL
_OCA = _OG + GLA_VAL
_OCB = _OCA + CONV_CH
_OGR = _OCB + CONV_CH

_LEVELS = (32, 16, 8, 4, 2, 1)


def _rms(x, w):
    return x * lax.rsqrt(jnp.mean(x * x, axis=-1, keepdims=True) + EPS) * w


def _bf(x):
    return x.astype(jnp.bfloat16)


def _dot(a, b):
    return jnp.dot(a, b, preferred_element_type=jnp.float32)


def _boundary(b, hs, rows):
    n, c = b.shape
    blk = 2 * hs
    if blk >= 8:
        b3 = b.reshape(n // blk, blk, c)
        return jnp.broadcast_to(b3[:, hs - 1:hs, :], (n // blk, blk, c)).reshape(n, c)
    y = pltpu.roll(b, n - (hs - 1), 0) if hs > 1 else b
    s = 1
    while s < blk:
        y = jnp.where((rows & s) != 0, pltpu.roll(y, s, 0), y)
        s *= 2
    return y


def _mixer_kernel(x_ref, ln_ref, win_ref, wgu_ref, bg_ref, gnw_ref, cw_ref, cb_ref, cnw_ref, cnb_ref,
                  wout_ref, o_ref, s_ref, ubuf_ref, ush_ref, y_ref):
    tt = x_ref.shape[1]
    nch = tt // CHUNK
    t = pl.program_id(1)

    @pl.when(t == 0)
    def _():
        s_ref[...] = jnp.zeros_like(s_ref)
        ubuf_ref[0:CONV_HALO, :] = jnp.zeros((CONV_HALO, CONV_CH), jnp.float32)

    x = x_ref[0]
    h = _bf(_rms(x, ln_ref[...]))
    z = _dot(h, win_ref[...])

    q = z[:, _OQ:_OQ + GLA_KEY] * (GLA_DK ** -0.5)
    k = z[:, _OK:_OK + GLA_KEY]
    v = z[:, _OV:_OV + GLA_VAL]
    g = z[:, _OG:_OG + GLA_VAL]
    ca = z[:, _OCA:_OCA + CONV_CH]
    cb = z[:, _OCB:_OCB + CONV_CH]
    gr = z[:, _OGR:_OGR + GATE_PAD]

    ubuf_ref[CONV_HALO:CONV_HALO + tt, :] = ca * jax.nn.sigmoid(cb)
    off0 = CONV_HALO - (CONV_WIDTH - 1)
    for r in range(1, SUBLANES):
        ush_ref[r - 1] = ubuf_ref[r:r + CONV_HALO + tt - SUBLANES, :]
    for r0 in range(0, tt, CHUNK):
        for c0 in range(0, CONV_CH, LANES):
            acc = jnp.broadcast_to(cb_ref[:, c0:c0 + LANES], (CHUNK, LANES))
            for j in range(CONV_WIDTH):
                r = (off0 + j) % SUBLANES
                a0 = r0 + off0 + j - r
                src = ubuf_ref if r == 0 else ush_ref.at[r - 1]
                acc = acc + cw_ref[j:j + 1, c0:c0 + LANES] * src[a0:a0 + CHUNK, c0:c0 + LANES]
            y_ref[r0:r0 + CHUNK, c0:c0 + LANES] = acc
    ubuf_ref[0:CONV_HALO, :] = ubuf_ref[tt:tt + CONV_HALO, :]
    yc = y_ref[...]
    mu = jnp.mean(yc, axis=-1, keepdims=True)
    yd = yc - mu
    var = jnp.mean(yd * yd, axis=-1, keepdims=True)
    u = yd * lax.rsqrt(var + EPS) * cnw_ref[...] + cnb_ref[...]
    u = u * jax.nn.sigmoid(u)

    logit = _dot(_bf(gr), wgu_ref[...]) + bg_ref[...]
    la = jax.nn.log_sigmoid(logit) * (1.0 / GATE_NORMALIZER)
    rows = lax.broadcasted_iota(jnp.int32, (tt, GLA_KEY), 0)
    rc = rows & (CHUNK - 1)
    b = la
    s = 1
    while s < CHUNK:
        b = b + jnp.where(rc >= s, pltpu.roll(b, s, 0), 0.0)
        s *= 2
    b3 = b.reshape(nch, CHUNK, GLA_KEY)
    blast3 = b3[:, CHUNK - 1:CHUNK, :]
    blast = jnp.broadcast_to(blast3, (nch, CHUNK, GLA_KEY)).reshape(tt, GLA_KEY)
    qe = _bf(q * jnp.exp(b))
    kl = _bf(k * jnp.exp(blast - b))
    vb = _bf(v)
    dl = jnp.exp(blast3.reshape(nch, GLA_KEY))
    dl_t = jnp.transpose(jnp.concatenate([dl, jnp.zeros((LANES - nch, GLA_KEY), jnp.float32)], axis=0))

    qh, kh = [_bf(q)], [_bf(k)]
    for hs in _LEVELS:
        e = jnp.exp(-jnp.abs(b - _boundary(b, hs, rows)))
        qh.append(_bf(q * e))
        kh.append(_bf(k * e))

    ii = lax.broadcasted_iota(jnp.int32, (CHUNK, GLA_HEADS * CHUNK), 0)
    jj = lax.broadcasted_iota(jnp.int32, (CHUNK, GLA_HEADS * CHUNK), 1) & (CHUNK - 1)
    masks = [ii == jj]
    for hs in _LEVELS:
        blk = 2 * hs
        masks.append(((ii // blk) == (jj // blk)) & ((ii & (blk - 1)) >= hs) & ((jj & (blk - 1)) < hs))
    rk = lax.broadcasted_iota(jnp.int32, (GLA_HEADS * CHUNK, GLA_KEY), 0) // CHUNK
    ck = lax.broadcasted_iota(jnp.int32, (GLA_HEADS * CHUNK, GLA_KEY), 1) // GLA_DK
    bd_k = (rk == ck).astype(jnp.bfloat16)
    rv = lax.broadcasted_iota(jnp.int32, (GLA_HEADS * CHUNK, GLA_VAL), 0) // CHUNK
    cv = lax.broadcasted_iota(jnp.int32, (GLA_HEADS * CHUNK, GLA_VAL), 1) // GLA_DV
    bd_v = rv == cv
    bd_vb = bd_v.astype(jnp.bfloat16)

    st = s_ref[...]
    o_parts = []
    for c in range(nch):
        sl = slice(c * CHUNK, (c + 1) * CHUNK)
        att = jnp.zeros((CHUNK, GLA_HEADS * CHUNK), jnp.float32)
        for lvl in range(len(masks)):
            kbd = jnp.concatenate([kh[lvl][sl]] * GLA_HEADS, axis=0) * bd_k
            sc = lax.dot_general(qh[lvl][sl], kbd, (((1,), (1,)), ((), ())),
                                 preferred_element_type=jnp.float32)
            att = jnp.where(masks[lvl], sc, att)
        vbd = jnp.concatenate([vb[sl]] * GLA_HEADS, axis=0) * bd_vb
        o_parts.append(_dot(_bf(att), vbd) + _dot(qe[sl], _bf(st)))
        upd = lax.dot_general(kl[sl], vb[sl], (((0,), (0,)), ((), ())), preferred_element_type=jnp.float32)
        st = st * dl_t[:, c:c + 1] + jnp.where(bd_v, upd, 0.0)
    s_ref[...] = st
    o = jnp.concatenate(o_parts, axis=0)

    gnw = gnw_ref[...]
    heads = []
    for hd in range(GLA_HEADS):
        oh = o[:, hd * GLA_DV:(hd + 1) * GLA_DV]
        heads.append(_rms(oh, gnw))
    o = jnp.concatenate(heads, axis=-1) * (g * jax.nn.sigmoid(g))

    mix = _bf(jnp.concatenate([o, u], axis=-1))
    o_ref[0] = x + _dot(mix, wout_ref[...])


def _mixer_call(x, ln_w, w_in_p, wgu_p, b_gate, gnw, conv_w, conv_b, cn_w, cn_b, w_out_b):
    bsz, seq, d = x.shape
    tt = min(TIME_TILE, seq)
    assert seq % tt == 0 and tt % CHUNK == 0
    const = lambda *shape: pl.BlockSpec(shape, lambda b, t: (0,) * len(shape))
    return pl.pallas_call(
        _mixer_kernel,
        out_shape=jax.ShapeDtypeStruct(x.shape, x.dtype),
        grid=(bsz, seq // tt),
        in_specs=[
            pl.BlockSpec((1, tt, d), lambda b, t: (b, t, 0)),
            const(1, d), const(d, IN_COLS_P), const(GATE_PAD, GLA_KEY), const(1, GLA_KEY), const(1, GLA_DV),
            const(CONV_HALO, CONV_CH), const(1, CONV_CH), const(1, CONV_CH), const(1, CONV_CH),
            const(GLA_VAL + CONV_CH, d),
        ],
        out_specs=pl.BlockSpec((1, tt, d), lambda b, t: (b, t, 0)),
        scratch_shapes=[
            pltpu.VMEM((GLA_HEADS * GLA_DK, GLA_VAL), jnp.float32),
            pltpu.VMEM((CONV_HALO + tt, CONV_CH), jnp.float32),
            pltpu.VMEM((SUBLANES - 1, CONV_HALO + tt - SUBLANES, CONV_CH), jnp.float32),
            pltpu.VMEM((tt, CONV_CH), jnp.float32),
        ],
        compiler_params=pltpu.CompilerParams(dimension_semantics=("arbitrary", "arbitrary"),
                                             vmem_limit_bytes=VMEM_LIMIT),
        name="mixer",
    )(x, ln_w, w_in_p, wgu_p, b_gate, gnw, conv_w, conv_b, cn_w, cn_b, w_out_b)


def _swiglu_chunk(h, wg_ref, wu_ref, wd_ref, f0):
    gt = _dot(h, wg_ref[:, f0:f0 + FF_CHUNK])
    up = _dot(h, wu_ref[:, f0:f0 + FF_CHUNK])
    a = _bf(gt * jax.nn.sigmoid(gt) * up)
    return _dot(a, wd_ref[f0:f0 + FF_CHUNK, :])


def _swiglu(h, wg_ref, wu_ref, wd_ref):
    y = None
    for f0 in range(0, D_FF, FF_CHUNK):
        part = _swiglu_chunk(h, wg_ref, wu_ref, wd_ref, f0)
        y = part if y is None else y + part
    return y


def _ffn_dense_kernel(x_ref, ln_ref, fin_ref, wg_ref, wu_ref, wd_ref, o_ref, *, final_norm):
    x = x_ref[...]
    out = x + _swiglu(_bf(_rms(x, ln_ref[...])), wg_ref.at[0], wu_ref.at[0], wd_ref.at[0])
    if final_norm:
        out = _rms(out, fin_ref[...])
    o_ref[...] = out


def _ffn_dense_call(x2, ln_w, fin_w, wg, wu, wd, *, final_norm):
    n, d = x2.shape
    tm = min(ROW_TILE, n)
    assert n % tm == 0
    return pl.pallas_call(
        functools.partial(_ffn_dense_kernel, final_norm=final_norm),
        out_shape=jax.ShapeDtypeStruct(x2.shape, x2.dtype),
        grid=(n // tm,),
        in_specs=[
            pl.BlockSpec((tm, d), lambda i: (i, 0)),
            pl.BlockSpec((1, d), lambda i: (0, 0)),
            pl.BlockSpec((1, d), lambda i: (0, 0)),
            pl.BlockSpec((1, d, D_FF), lambda i: (0, 0, 0)),
            pl.BlockSpec((1, d, D_FF), lambda i: (0, 0, 0)),
            pl.BlockSpec((1, D_FF, d), lambda i: (0, 0, 0)),
        ],
        out_specs=pl.BlockSpec((tm, d), lambda i: (i, 0)),
        compiler_params=pltpu.CompilerParams(dimension_semantics=("arbitrary",), vmem_limit_bytes=VMEM_LIMIT),
        name="ffn_dense",
    )(x2, ln_w, fin_w, wg, wu, wd)


def _store_records(rec_ref, vals):
    m = vals.shape[0]
    for s in range(REC_ROWS):
        rec_ref[pl.ds(s, m, stride=REC_ROWS), :] = vals[:, s * LANES:(s + 1) * LANES]


def _load_records(rec_ref, m):
    return [rec_ref[pl.ds(s, m, stride=REC_ROWS), :] for s in range(REC_ROWS)]


def _copy_records(src_ref, src_off, dst_ref, dst_off, n, sem, max_rows, wait=False):
    bit = max_rows.bit_length() - 1
    while bit >= 0:
        size = (1 << bit) * REC_ROWS
        done = lax.shift_left(lax.shift_right_logical(n, bit + 1), bit + 1)
        src0 = 0 if src_off is None else pl.multiple_of((src_off + done) * REC_ROWS, REC_ROWS)
        dst0 = pl.multiple_of((dst_off + done) * REC_ROWS, REC_ROWS)

        @pl.when((lax.shift_right_logical(n, bit) & 1) == 1)
        def _(size=size, src0=src0, dst0=dst0):
            cp = pltpu.make_async_copy(src_ref.at[pl.ds(src0, size)], dst_ref.at[pl.ds(dst0, size)], sem)
            cp.wait() if wait else cp.start()
        bit -= 1


def _one_hot_rows(pos0, pos1, n_rows):
    r = lax.broadcasted_iota(jnp.int32, (n_rows, pos0.shape[1]), 0)
    return jnp.where((r == pos0) | (r == pos1), 1.0, 0.0).astype(jnp.bfloat16)


def _route_kernel(x_ref, ln_ref, wrh_ref, wrl_ref, pos_ref, gate_ref, cnt_ref):
    ts = x_ref.shape[0]
    t = pl.program_id(0)
    h = _rms(x_ref[...], ln_ref[...])
    h_hi = _bf(h)
    h_lo = _bf(h - h_hi.astype(jnp.float32))
    logits = _dot(h_hi, wrh_ref[...]) + (_dot(h_hi, wrl_ref[...]) + _dot(h_lo, wrh_ref[...]))
    lt = jnp.transpose(logits)[0:N_EXPERTS, :]
    row = lax.broadcasted_iota(jnp.int32, lt.shape, 0)
    neg = jnp.float32(-jnp.inf)
    m1 = jnp.max(lt, axis=0, keepdims=True)
    i1 = jnp.min(jnp.where(lt == m1, row, N_EXPERTS), axis=0, keepdims=True)
    rest = jnp.where(row == i1, neg, lt)
    m2 = jnp.max(rest, axis=0, keepdims=True)
    i2 = jnp.min(jnp.where(rest == m2, row, N_EXPERTS), axis=0, keepdims=True)
    e2 = jnp.exp(m2 - m1)
    den = 1.0 + e2
    g1 = 1.0 / den
    g2 = e2 / den
    sel1 = row == i1
    sel2 = row == i2
    oh = jnp.where(sel1 | sel2, 1.0, 0.0)

    sp = lax.broadcasted_iota(jnp.int32, (ts, ts), 0)
    sc = lax.broadcasted_iota(jnp.int32, (ts, ts), 1)
    upper = jnp.where(sp < sc, 1.0, 0.0).astype(jnp.bfloat16)
    rank = _dot(_bf(oh), upper)
    rk1 = jnp.sum(jnp.where(sel1, rank, 0.0), axis=0, keepdims=True)
    rk2 = jnp.sum(jnp.where(sel2, rank, 0.0), axis=0, keepdims=True)

    cnts, offs = [], []
    off = jnp.int32(0)
    for e in range(N_EXPERTS):
        c = jnp.sum(oh[e:e + 1, :]).astype(jnp.int32)
        cnts.append(c)
        offs.append(off)
        off = off + c
    off1 = jnp.zeros_like(rk1)
    off2 = jnp.zeros_like(rk2)
    for e in range(N_EXPERTS):
        fe = offs[e].astype(jnp.float32)
        off1 = jnp.where(i1 == e, fe, off1)
        off2 = jnp.where(i2 == e, fe, off2)
    pos1 = (off1 + rk1).astype(jnp.int32)
    pos2 = (off2 + rk2).astype(jnp.int32)
    pos_ref[0] = jnp.concatenate([pos1, pos2, jnp.zeros((SUBLANES - TOP_K, ts), jnp.int32)], axis=0)
    gate_ref[0] = jnp.concatenate([g1, g2, jnp.zeros((SUBLANES - TOP_K, ts), jnp.float32)], axis=0)
    for e in range(N_EXPERTS):
        cnt_ref[t * N_EXPERTS + e] = cnts[e]


def _route_call(x2, ln_w, wr_hi, wr_lo):
    n, d = x2.shape
    ts = min(ROW_TILE, n)
    nt = n // ts
    return pl.pallas_call(
        _route_kernel,
        out_shape=(
            jax.ShapeDtypeStruct((nt, SUBLANES, ts), jnp.int32),
            jax.ShapeDtypeStruct((nt, SUBLANES, ts), jnp.float32),
            jax.ShapeDtypeStruct((nt * N_EXPERTS,), jnp.int32),
        ),
        grid=(nt,),
        in_specs=[
            pl.BlockSpec((ts, d), lambda i: (i, 0)),
            pl.BlockSpec((1, d), lambda i: (0, 0)),
            pl.BlockSpec((d, LANES), lambda i: (0, 0)),
            pl.BlockSpec((d, LANES), lambda i: (0, 0)),
        ],
        out_specs=(
            pl.BlockSpec((1, SUBLANES, ts), lambda i: (i, 0, 0)),
            pl.BlockSpec((1, SUBLANES, ts), lambda i: (i, 0, 0)),
            pl.BlockSpec(memory_space=pltpu.SMEM),
        ),
        compiler_params=pltpu.CompilerParams(dimension_semantics=("arbitrary",), vmem_limit_bytes=VMEM_LIMIT),
        name="route",
    )(x2, ln_w, wr_hi, wr_lo)


def _dispatch_kernel(seg_ref, cnt_ref, zstart_ref, zlen_ref, nval_ref, x_ref, ln_ref, pos_ref, xs_hbm,
                     stage_ref, zero_ref, sem, zsem, *, max_tiles):
    ts = x_ref.shape[0]
    n_sorted = TOP_K * ts
    t = pl.program_id(0)
    nt = pl.num_programs(0)
    h = _bf(_rms(x_ref[...], ln_ref[...]))
    pos = pos_ref[0]
    perm = _one_hot_rows(pos[0:1, :], pos[1:2, :], n_sorted)
    sorted_h = _dot(perm, h)

    @pl.when(t > 0)
    def _():
        pltpu.make_async_copy(stage_ref, xs_hbm.at[pl.ds(0, n_sorted * REC_ROWS)], sem).wait()

    _store_records(stage_ref, sorted_h)

    off = jnp.int32(0)
    for e in range(N_EXPERTS):
        c = cnt_ref[t * N_EXPERTS + e]
        _copy_records(stage_ref, off, xs_hbm, seg_ref[t * N_EXPERTS + e], c, sem, ts)
        off = off + c

    @pl.when(t == nt - 1)
    def _():
        pltpu.make_async_copy(stage_ref, xs_hbm.at[pl.ds(0, n_sorted * REC_ROWS)], sem).wait()
        tm = zero_ref.shape[0] // REC_ROWS
        zero_ref[...] = jnp.zeros_like(zero_ref)

        def tail_fill(k):
            return pltpu.make_async_copy(
                zero_ref, xs_hbm.at[pl.ds(pl.multiple_of((nval_ref[0] + k) * tm * REC_ROWS, REC_ROWS),
                                          tm * REC_ROWS)], zsem)

        for wait in (False, True):
            for e in range(N_EXPERTS):
                _copy_records(zero_ref, None, xs_hbm, zstart_ref[e], zlen_ref[e], zsem, tm, wait=wait)
            for k in range(N_EXPERTS):
                @pl.when(nval_ref[0] + k < max_tiles)
                def _(k=k, wait=wait):
                    tail_fill(k).wait() if wait else tail_fill(k).start()


def _dispatch_call(seg, cnt, zstart, zlen, n_valid, x2, ln_w, pos, max_tiles):
    n, d = x2.shape
    ts = min(ROW_TILE, n)
    nt = n // ts
    return pl.pallas_call(
        functools.partial(_dispatch_kernel, max_tiles=max_tiles),
        out_shape=jax.ShapeDtypeStruct((max_tiles * ROW_TILE * REC_ROWS, LANES), jnp.float32),
        grid_spec=pltpu.PrefetchScalarGridSpec(
            num_scalar_prefetch=5,
            grid=(nt,),
            in_specs=[
                pl.BlockSpec((ts, d), lambda i, *_: (i, 0)),
                pl.BlockSpec((1, d), lambda i, *_: (0, 0)),
                pl.BlockSpec((1, SUBLANES, ts), lambda i, *_: (i, 0, 0)),
            ],
            out_specs=pl.BlockSpec(memory_space=pl.ANY),
            scratch_shapes=[
                pltpu.VMEM((TOP_K * ts * REC_ROWS, LANES), jnp.float32),
                pltpu.VMEM((ROW_TILE * REC_ROWS, LANES), jnp.float32),
                pltpu.SemaphoreType.DMA(()),
                pltpu.SemaphoreType.DMA(()),
            ],
        ),
        compiler_params=pltpu.CompilerParams(dimension_semantics=("arbitrary",), vmem_limit_bytes=VMEM_LIMIT),
        name="dispatch",
    )(seg, cnt, zstart, zlen, n_valid, x2, ln_w, pos)


def _ffn_group_kernel(texp_ref, tnext_ref, tswap_ref, nval_ref, xs_ref, wg_hbm, wu_hbm, wd_hbm, ys_ref,
                      wg_ref, wu_ref, wd_ref, sg_ref, su_ref, sd_ref, sems):
    j = pl.program_id(0)
    n_chunks = D_FF // FF_CHUNK

    def chunk_copies(e, f, slot):
        c0 = f * FF_CHUNK
        return (pltpu.make_async_copy(wg_hbm.at[e, :, pl.ds(c0, FF_CHUNK)], sg_ref.at[slot], sems.at[slot]),
                pltpu.make_async_copy(wu_hbm.at[e, :, pl.ds(c0, FF_CHUNK)], su_ref.at[slot], sems.at[slot]),
                pltpu.make_async_copy(wd_hbm.at[e, pl.ds(c0, FF_CHUNK), :], sd_ref.at[slot], sems.at[slot]))

    def start(e, f):
        for cp in chunk_copies(e, f, f % 2):
            cp.start()

    def install(e, f):
        for cp in chunk_copies(e, f, f % 2):
            cp.wait()
        c0 = f * FF_CHUNK
        wg_ref[:, c0:c0 + FF_CHUNK] = _bf(sg_ref[f % 2])
        wu_ref[:, c0:c0 + FF_CHUNK] = _bf(su_ref[f % 2])
        wd_ref[c0:c0 + FF_CHUNK, :] = _bf(sd_ref[f % 2])

    def load_h():
        tm = xs_ref.shape[0] // REC_ROWS
        return jnp.concatenate([_bf(w) for w in _load_records(xs_ref, tm)], axis=1)

    @pl.when(j == 0)
    def _():
        e = texp_ref[0]
        start(e, 0)
        for f in range(n_chunks):
            if f + 1 < n_chunks:
                start(e, f + 1)
            install(e, f)

    valid = j < nval_ref[0]
    swap = tswap_ref[j] == 1

    @pl.when(jnp.logical_not(valid))
    def _():
        ys_ref[...] = jnp.zeros_like(ys_ref)

    @pl.when(valid & jnp.logical_not(swap))
    def _():
        _store_records(ys_ref, _swiglu(load_h(), wg_ref, wu_ref, wd_ref))

    @pl.when(valid & swap)
    def _():
        e = tnext_ref[j]
        start(e, 0)
        start(e, 1)
        h = load_h()
        y = None
        for f in range(n_chunks):
            part = _swiglu_chunk(h, wg_ref, wu_ref, wd_ref, f * FF_CHUNK)
            y = part if y is None else y + part
            install(e, f)
            if f + 2 < n_chunks:
                start(e, f + 2)
        _store_records(ys_ref, y)


def _ffn_group_call(tile_expert, tile_next, tile_swap, n_valid, xs, wg, wu, wd):
    rows = xs.shape[0]
    tm = ROW_TILE
    n_tiles = tile_expert.shape[0]
    d = wg.shape[1]
    return pl.pallas_call(
        _ffn_group_kernel,
        out_shape=jax.ShapeDtypeStruct((rows, LANES), jnp.float32),
        grid_spec=pltpu.PrefetchScalarGridSpec(
            num_scalar_prefetch=4,
            grid=(n_tiles,),
            in_specs=[
                pl.BlockSpec((tm * REC_ROWS, LANES), lambda j, *_: (j, 0)),
                pl.BlockSpec(memory_space=pl.ANY),
                pl.BlockSpec(memory_space=pl.ANY),
                pl.BlockSpec(memory_space=pl.ANY),
            ],
            out_specs=pl.BlockSpec((tm * REC_ROWS, LANES), lambda j, *_: (j, 0)),
            scratch_shapes=[
                pltpu.VMEM((d, D_FF), jnp.bfloat16),
                pltpu.VMEM((d, D_FF), jnp.bfloat16),
                pltpu.VMEM((D_FF, d), jnp.bfloat16),
                pltpu.VMEM((2, d, FF_CHUNK), jnp.float32),
                pltpu.VMEM((2, d, FF_CHUNK), jnp.float32),
                pltpu.VMEM((2, FF_CHUNK, d), jnp.float32),
                pltpu.SemaphoreType.DMA((2,)),
            ],
        ),
        compiler_params=pltpu.CompilerParams(dimension_semantics=("arbitrary",), vmem_limit_bytes=VMEM_LIMIT),
        name="ffn_group",
    )(tile_expert, tile_next, tile_swap, n_valid, xs, wg, wu, wd)


def _combine_kernel(seg_ref, cnt_ref, x_ref, fin_ref, pos_ref, gate_ref, ys_hbm, o_ref, ybuf_ref, sems,
                    *, final_norm):
    ts = x_ref.shape[0]
    n_sorted = TOP_K * ts
    t = pl.program_id(0)
    nt = pl.num_programs(0)

    def fetch(tile, slot):
        off = jnp.int32(0)
        for e in range(N_EXPERTS):
            c = cnt_ref[tile * N_EXPERTS + e]
            _copy_records(ys_hbm, seg_ref[tile * N_EXPERTS + e], ybuf_ref.at[slot], off, c, sems.at[slot], ts)
            off = off + c

    @pl.when(t == 0)
    def _():
        fetch(0, 0)

    @pl.when(t + 1 < nt)
    def _():
        fetch(t + 1, (t + 1) % 2)

    slot = t % 2
    pltpu.make_async_copy(ys_hbm.at[pl.ds(0, n_sorted * REC_ROWS)], ybuf_ref.at[slot], sems.at[slot]).wait()
    ys = jnp.concatenate([_bf(w) for w in _load_records(ybuf_ref.at[slot], n_sorted)], axis=1)
    pos = pos_ref[0]
    gates = gate_ref[0]
    r = lax.broadcasted_iota(jnp.int32, (n_sorted, ts), 0)
    sel = _bf(jnp.where(r == pos[0:1, :], gates[0:1, :], 0.0) + jnp.where(r == pos[1:2, :], gates[1:2, :], 0.0))
    y = lax.dot_general(sel, ys, (((0,), (0,)), ((), ())), preferred_element_type=jnp.float32)
    out = x_ref[...] + y
    if final_norm:
        out = _rms(out, fin_ref[...])
    o_ref[...] = out


def _combine_call(seg, cnt, x2, fin_w, pos, gates, ys, *, final_norm):
    n, d = x2.shape
    ts = min(ROW_TILE, n)
    nt = n // ts
    return pl.pallas_call(
        functools.partial(_combine_kernel, final_norm=final_norm),
        out_shape=jax.ShapeDtypeStruct(x2.shape, x2.dtype),
        grid_spec=pltpu.PrefetchScalarGridSpec(
            num_scalar_prefetch=2,
            grid=(nt,),
            in_specs=[
                pl.BlockSpec((ts, d), lambda i, sg, ct: (i, 0)),
                pl.BlockSpec((1, d), lambda i, sg, ct: (0, 0)),
                pl.BlockSpec((1, SUBLANES, ts), lambda i, sg, ct: (i, 0, 0)),
                pl.BlockSpec((1, SUBLANES, ts), lambda i, sg, ct: (i, 0, 0)),
                pl.BlockSpec(memory_space=pl.ANY),
            ],
            out_specs=pl.BlockSpec((ts, d), lambda i, sg, ct: (i, 0)),
            scratch_shapes=[
                pltpu.VMEM((2, TOP_K * ts * REC_ROWS, LANES), jnp.float32),
                pltpu.SemaphoreType.DMA((2,)),
            ],
        ),
        compiler_params=pltpu.CompilerParams(dimension_semantics=("arbitrary",), vmem_limit_bytes=VMEM_LIMIT),
        name="combine",
    )(seg, cnt, x2, fin_w, pos, gates, ys)


def _moe(x2, ln_w, fin_w, w_router, wg, wu, wd, *, final_norm):
    n, d = x2.shape
    tm = ROW_TILE
    wr_p = jnp.concatenate([w_router, jnp.zeros((d, LANES - N_EXPERTS), w_router.dtype)], axis=1)
    wr_hi = _bf(wr_p)
    wr_lo = _bf(wr_p - wr_hi.astype(jnp.float32))
    pos, gates, cnt = _route_call(x2, ln_w, wr_hi, wr_lo)

    i32 = jnp.int32
    nt = cnt.shape[0] // N_EXPERTS
    cnt2 = cnt.reshape(nt, N_EXPERTS)
    totals = jnp.sum(cnt2, axis=0)
    tiles_e = (totals + tm - 1) // tm
    ends = jnp.cumsum(tiles_e)
    base = (ends - tiles_e) * tm
    seg = (base[None, :] + jnp.cumsum(cnt2, axis=0) - cnt2).reshape(-1).astype(i32)
    zstart = (base + totals).astype(i32)
    zlen = (tiles_e * tm - totals).astype(i32)
    n_valid = ends[-1].reshape(1).astype(i32)
    max_tiles = (TOP_K * n) // tm + N_EXPERTS
    j = jnp.minimum(jnp.arange(max_tiles, dtype=i32), n_valid - 1)
    tile_expert = jnp.sum((j[:, None] >= ends[None, :]).astype(i32), axis=1).astype(i32)
    tile_next = jnp.concatenate([tile_expert[1:], tile_expert[-1:]])
    tile_swap = (tile_next != tile_expert).astype(i32)

    xs = _dispatch_call(seg, cnt, zstart, zlen, n_valid, x2, ln_w, pos, max_tiles)
    ys = _ffn_group_call(tile_expert, tile_next, tile_swap, n_valid, xs, wg, wu, wd)
    return _combine_call(seg, cnt, x2, fin_w, pos, gates, ys, final_norm=final_norm)


def _prep_w_in(w):
    d = w.shape[0]
    o_gr = 2 * GLA_KEY + 2 * GLA_VAL
    parts = [w[:, :o_gr], w[:, o_gr + GATE_RANK:], w[:, o_gr:o_gr + GATE_RANK],
             jnp.zeros((d, GATE_PAD - GATE_RANK), w.dtype)]
    return _bf(jnp.concatenate(parts, axis=1))


def kernel(x, ln1_w, w_in, w_gate_up, b_gate, gla_norm_w, conv_w, conv_b, cn_w, cn_b, w_out, ln2_w, wd_gate,
           wd_up, wd_down, w_router, we_gate, we_up, we_down, final_norm_w):
    bsz, seq, d = x.shape
    depth = ln1_w.shape[0]
    n = bsz * seq
    row = lambda a: a.reshape(1, -1)
    for l in range(depth):
        wgu_p = _bf(jnp.concatenate(
            [w_gate_up[l], jnp.zeros((GATE_PAD - GATE_RANK, GLA_KEY), w_gate_up.dtype)], axis=0))
        cw_p = jnp.concatenate([conv_w[l], jnp.zeros((CONV_HALO - CONV_WIDTH, CONV_CH), conv_w.dtype)], axis=0)
        x = _mixer_call(x, row(ln1_w[l]), _prep_w_in(w_in[l]), wgu_p, row(b_gate[l]), row(gla_norm_w[l]),
                        cw_p, row(conv_b[l]), row(cn_w[l]), row(cn_b[l]), _bf(w_out[l]))
        x2 = x.reshape(n, d)
        last = l == depth - 1
        i = l // 2
        if l % 2 == 0:
            x2 = _ffn_dense_call(x2, row(ln2_w[l]), row(final_norm_w), _bf(wd_gate[i:i + 1]),
                                 _bf(wd_up[i:i + 1]), _bf(wd_down[i:i + 1]), final_norm=last)
        else:
            x2 = _moe(x2, row(ln2_w[l]), row(final_norm_w), w_router[i], we_gate[i], we_up[i], we_down[i],
                      final_norm=last)
        x = x2.reshape(bsz, seq, d)
    return x
```

```python
import functools

import jax
import jax.numpy as jnp
from jax import lax
from jax.experimental import pallas as pl
from jax.experimental.pallas import tpu as pltpu

D_MODEL = 1024
GLA_HEADS = 4
GLA_DK = 64
GLA_DV = 128
GLA_KEY = GLA_HEADS * GLA_DK
GLA_VAL = GLA_HEADS * GLA_DV
GATE_RANK = 16
GATE_NORMALIZER = 16.0
CONV_CH = 512
CONV_WIDTH = 31
D_FF = 2816
N_EXPERTS = 8
TOP_K = 2
EPS = 1e-6

LANES = 128
CHUNK = 64
TIME_TILE = 512
ROW_TILE = 512
FF_CHUNK = 256
SWAP_SPLIT = 6
CONV_HALO = 32
GATE_PAD = LANES
IN_COLS_P = 2 * GLA_KEY + 2 * GLA_VAL + 2 * CONV_CH + GATE_PAD
VMEM_LIMIT = 56 * 1024 * 1024
SUBLANES = 8
REC_ROWS = D_MODEL // LANES

_OQ, _OK, _OV, _OG = 0, GLA_KEY, 2 * GLA_KEY, 2 * GLA_KEY + GLA_VAL
_OCA = _OG + GLA_VAL
_OCB = _OCA + CONV_CH
_OGR = _OCB + CONV_CH

_LEVELS = (32, 16, 8, 4, 2, 1)


def _rms(x, w):
    return x * lax.rsqrt(jnp.mean(x * x, axis=-1, keepdims=True) + EPS) * w


def _bf(x):
    return x.astype(jnp.bfloat16)


def _dot(a, b):
    return jnp.dot(a, b, preferred_element_type=jnp.float32)


def _boundary(b, hs, rows):
    n, c = b.shape
    blk = 2 * hs
    if blk >= 8:
        b3 = b.reshape(n // blk, blk, c)
        return jnp.broadcast_to(b3[:, hs - 1:hs, :], (n // blk, blk, c)).reshape(n, c)
    y = pltpu.roll(b, n - (hs - 1), 0) if hs > 1 else b
    s = 1
    while s < blk:
        y = jnp.where((rows & s) != 0, pltpu.roll(y, s, 0), y)
        s *= 2
    return y


def _mixer_kernel(x_ref, ln_ref, win_ref, wgu_ref, bg_ref, gnw_ref, cw_ref, cb_ref, cnw_ref, cnb_ref,
                  wout_ref, o_ref, s_ref, ubuf_ref, ush_ref, y_ref):
    tt = x_ref.shape[1]
    nch = tt // CHUNK
    t = pl.program_id(1)

    @pl.when(t == 0)
    def _():
        s_ref[...] = jnp.zeros_like(s_ref)
        ubuf_ref[0:CONV_HALO, :] = jnp.zeros((CONV_HALO, CONV_CH), jnp.float32)

    x = x_ref[0]
    h = _bf(_rms(x, ln_ref[...]))
    z = _dot(h, win_ref[...])

    q = z[:, _OQ:_OQ + GLA_KEY] * (GLA_DK ** -0.5)
    k = z[:, _OK:_OK + GLA_KEY]
    v = z[:, _OV:_OV + GLA_VAL]
    g = z[:, _OG:_OG + GLA_VAL]
    ca = z[:, _OCA:_OCA + CONV_CH]
    cb = z[:, _OCB:_OCB + CONV_CH]
    gr = z[:, _OGR:_OGR + GATE_PAD]

    ubuf_ref[CONV_HALO:CONV_HALO + tt, :] = ca * jax.nn.sigmoid(cb)
    off0 = CONV_HALO - (CONV_WIDTH - 1)
    ufull = ubuf_ref[...]
    for r in range(1, SUBLANES):
        ush_ref[r - 1] = pltpu.roll(ufull, CONV_HALO + tt - r, 0)[0:CONV_HALO + tt - SUBLANES, :]
    for r0 in range(0, tt, CHUNK):
        for c0 in range(0, CONV_CH, LANES):
            acc = jnp.broadcast_to(cb_ref[:, c0:c0 + LANES], (CHUNK, LANES))
            for j in range(CONV_WIDTH):
                r = (off0 + j) % SUBLANES
                a0 = r0 + off0 + j - r
                src = ubuf_ref if r == 0 else ush_ref.at[r - 1]
                acc = acc + cw_ref[j:j + 1, c0:c0 + LANES] * src[a0:a0 + CHUNK, c0:c0 + LANES]
            y_ref[r0:r0 + CHUNK, c0:c0 + LANES] = acc
    ubuf_ref[0:CONV_HALO, :] = ubuf_ref[tt:tt + CONV_HALO, :]
    yc = y_ref[...]
    mu = jnp.mean(yc, axis=-1, keepdims=True)
    yd = yc - mu
    var = jnp.mean(yd * yd, axis=-1, keepdims=True)
    u = yd * lax.rsqrt(var + EPS) * cnw_ref[...] + cnb_ref[...]
    u = u * jax.nn.sigmoid(u)

    logit = _dot(_bf(gr), wgu_ref[...]) + bg_ref[...]
    la = jax.nn.log_sigmoid(logit) * (1.0 / GATE_NORMALIZER)
    rows = lax.broadcasted_iota(jnp.int32, (tt, GLA_KEY), 0)
    rc = rows & (CHUNK - 1)
    b = la
    s = 1
    while s < CHUNK:
        b = b + jnp.where(rc >= s, pltpu.roll(b, s, 0), 0.0)
        s *= 2
    b3 = b.reshape(nch, CHUNK, GLA_KEY)
    blast3 = b3[:, CHUNK - 1:CHUNK, :]
    blast = jnp.broadcast_to(blast3, (nch, CHUNK, GLA_KEY)).reshape(tt, GLA_KEY)
    qe = _bf(q * jnp.exp(b))
    kl = _bf(k * jnp.exp(blast - b))
    vb = _bf(v)
    dl = jnp.exp(blast3.reshape(nch, GLA_KEY))
    dl_t = jnp.transpose(jnp.concatenate([dl, jnp.zeros((LANES - nch, GLA_KEY), jnp.float32)], axis=0))

    qh, kh = [_bf(q)], [_bf(k)]
    for hs in _LEVELS:
        e = jnp.exp(-jnp.abs(b - _boundary(b, hs, rows)))
        qh.append(_bf(q * e))
        kh.append(_bf(k * e))

    ii = lax.broadcasted_iota(jnp.int32, (CHUNK, GLA_HEADS * CHUNK), 0)
    jj = lax.broadcasted_iota(jnp.int32, (CHUNK, GLA_HEADS * CHUNK), 1) & (CHUNK - 1)
    masks = [ii == jj]
    for hs in _LEVELS:
        blk = 2 * hs
        masks.append(((ii // blk) == (jj // blk)) & ((ii & (blk - 1)) >= hs) & ((jj & (blk - 1)) < hs))
    rk = lax.broadcasted_iota(jnp.int32, (GLA_HEADS * CHUNK, GLA_KEY), 0) // CHUNK
    ck = lax.broadcasted_iota(jnp.int32, (GLA_HEADS * CHUNK, GLA_KEY), 1) // GLA_DK
    bd_k = (rk == ck).astype(jnp.bfloat16)
    rv = lax.broadcasted_iota(jnp.int32, (GLA_HEADS * CHUNK, GLA_VAL), 0) // CHUNK
    cv = lax.broadcasted_iota(jnp.int32, (GLA_HEADS * CHUNK, GLA_VAL), 1) // GLA_DV
    bd_v = rv == cv
    bd_vb = bd_v.astype(jnp.bfloat16)

    st = s_ref[...]
    o_parts = []
    for c in range(nch):
        sl = slice(c * CHUNK, (c + 1) * CHUNK)
        att = jnp.zeros((CHUNK, GLA_HEADS * CHUNK), jnp.float32)
        for lvl in range(len(masks)):
            kbd = jnp.concatenate([kh[lvl][sl]] * GLA_HEADS, axis=0) * bd_k
            sc = lax.dot_general(qh[lvl][sl], kbd, (((1,), (1,)), ((), ())),
                                 preferred_element_type=jnp.float32)
            att = jnp.where(masks[lvl], sc, att)
        vbd = jnp.concatenate([vb[sl]] * GLA_HEADS, axis=0) * bd_vb
        o_parts.append(_dot(_bf(att), vbd) + _dot(qe[sl], _bf(st)))
        upd = lax.dot_general(kl[sl], vb[sl], (((0,), (0,)), ((), ())), preferred_element_type=jnp.float32)
        st = st * dl_t[:, c:c + 1] + jnp.where(bd_v, upd, 0.0)
    s_ref[...] = st
    o = jnp.concatenate(o_parts, axis=0)

    gnw = gnw_ref[...]
    heads = []
    for hd in range(GLA_HEADS):
        oh = o[:, hd * GLA_DV:(hd + 1) * GLA_DV]
        heads.append(_rms(oh, gnw))
    o = jnp.concatenate(heads, axis=-1) * (g * jax.nn.sigmoid(g))

    mix = _bf(jnp.concatenate([o, u], axis=-1))
    o_ref[0] = x + _dot(mix, wout_ref[...])


def _mixer_call(x, layer, ln_w, w_in_p, wgu_p, b_gate, gnw, conv_w, conv_b, cn_w, cn_b, w_out_b):
    bsz, seq, d = x.shape
    tt = min(TIME_TILE, seq)
    assert seq % tt == 0 and tt % CHUNK == 0
    const = lambda *shape: pl.BlockSpec((None,) + shape, lambda b, t: (layer,) + (0,) * len(shape))
    return pl.pallas_call(
        _mixer_kernel,
        out_shape=jax.ShapeDtypeStruct(x.shape, x.dtype),
        grid=(bsz, seq // tt),
        in_specs=[
            pl.BlockSpec((1, tt, d), lambda b, t: (b, t, 0)),
            const(1, d), const(d, IN_COLS_P), const(GATE_PAD, GLA_KEY), const(1, GLA_KEY), const(1, GLA_DV),
            const(CONV_HALO, CONV_CH), const(1, CONV_CH), const(1, CONV_CH), const(1, CONV_CH),
            const(GLA_VAL + CONV_CH, d),
        ],
        out_specs=pl.BlockSpec((1, tt, d), lambda b, t: (b, t, 0)),
        scratch_shapes=[
            pltpu.VMEM((GLA_HEADS * GLA_DK, GLA_VAL), jnp.float32),
            pltpu.VMEM((CONV_HALO + tt, CONV_CH), jnp.float32),
            pltpu.VMEM((SUBLANES - 1, CONV_HALO + tt - SUBLANES, CONV_CH), jnp.float32),
            pltpu.VMEM((tt, CONV_CH), jnp.float32),
        ],
        compiler_params=pltpu.CompilerParams(dimension_semantics=("arbitrary", "arbitrary"),
                                             vmem_limit_bytes=VMEM_LIMIT),
        name="mixer",
    )(x, ln_w, w_in_p, wgu_p, b_gate, gnw, conv_w, conv_b, cn_w, cn_b, w_out_b)


def _swiglu_chunk(h, wg_ref, wu_ref, wd_ref, f0):
    gt = _dot(h, wg_ref[:, f0:f0 + FF_CHUNK])
    up = _dot(h, wu_ref[:, f0:f0 + FF_CHUNK])
    a = _bf(gt * jax.nn.sigmoid(gt) * up)
    return _dot(a, wd_ref[f0:f0 + FF_CHUNK, :])


def _swiglu(h, wg_ref, wu_ref, wd_ref):
    y = None
    for f0 in range(0, D_FF, FF_CHUNK):
        part = _swiglu_chunk(h, wg_ref, wu_ref, wd_ref, f0)
        y = part if y is None else y + part
    return y


def _ffn_dense_kernel(x_ref, ln_ref, fin_ref, wg_ref, wu_ref, wd_ref, o_ref, *, final_norm):
    x = x_ref[...]
    out = x + _swiglu(_bf(_rms(x, ln_ref[...])), wg_ref.at[0], wu_ref.at[0], wd_ref.at[0])
    if final_norm:
        out = _rms(out, fin_ref[...])
    o_ref[...] = out


def _ffn_dense_call(x2, ln_w, fin_w, wg, wu, wd, *, final_norm):
    n, d = x2.shape
    tm = min(ROW_TILE, n)
    assert n % tm == 0
    return pl.pallas_call(
        functools.partial(_ffn_dense_kernel, final_norm=final_norm),
        out_shape=jax.ShapeDtypeStruct(x2.shape, x2.dtype),
        grid=(n // tm,),
        in_specs=[
            pl.BlockSpec((tm, d), lambda i: (i, 0)),
            pl.BlockSpec((1, d), lambda i: (0, 0)),
            pl.BlockSpec((1, d), lambda i: (0, 0)),
            pl.BlockSpec((1, d, D_FF), lambda i: (0, 0, 0)),
            pl.BlockSpec((1, d, D_FF), lambda i: (0, 0, 0)),
            pl.BlockSpec((1, D_FF, d), lambda i: (0, 0, 0)),
        ],
        out_specs=pl.BlockSpec((tm, d), lambda i: (i, 0)),
        compiler_params=pltpu.CompilerParams(dimension_semantics=("arbitrary",), vmem_limit_bytes=VMEM_LIMIT),
        name="ffn_dense",
    )(x2, ln_w, fin_w, wg, wu, wd)


def _store_records(rec_ref, vals):
    m = vals.shape[0]
    for s in range(REC_ROWS):
        rec_ref[pl.ds(s, m, stride=REC_ROWS), :] = vals[:, s * LANES:(s + 1) * LANES]


def _load_records(rec_ref, m):
    return [rec_ref[pl.ds(s, m, stride=REC_ROWS), :] for s in range(REC_ROWS)]


def _copy_records(src_ref, src_off, dst_ref, dst_off, n, sem, max_rows, wait=False):
    bit = max_rows.bit_length() - 1
    while bit >= 0:
        size = (1 << bit) * REC_ROWS
        done = lax.shift_left(lax.shift_right_logical(n, bit + 1), bit + 1)
        src0 = 0 if src_off is None else pl.multiple_of((src_off + done) * REC_ROWS, REC_ROWS)
        dst0 = pl.multiple_of((dst_off + done) * REC_ROWS, REC_ROWS)

        @pl.when((lax.shift_right_logical(n, bit) & 1) == 1)
        def _(size=size, src0=src0, dst0=dst0):
            cp = pltpu.make_async_copy(src_ref.at[pl.ds(src0, size)], dst_ref.at[pl.ds(dst0, size)], sem)
            cp.wait() if wait else cp.start()
        bit -= 1


def _one_hot_rows(pos0, pos1, n_rows):
    r = lax.broadcasted_iota(jnp.int32, (n_rows, pos0.shape[1]), 0)
    return jnp.where((r == pos0) | (r == pos1), 1.0, 0.0).astype(jnp.bfloat16)


def _route_kernel(x_ref, ln_ref, wrh_ref, wrl_ref, pos_ref, gate_ref, cnt_ref):
    ts = x_ref.shape[0]
    t = pl.program_id(0)
    h = _rms(x_ref[...], ln_ref[...])
    h_hi = _bf(h)
    h_lo = _bf(h - h_hi.astype(jnp.float32))
    logits = _dot(h_hi, wrh_ref[...]) + (_dot(h_hi, wrl_ref[...]) + _dot(h_lo, wrh_ref[...]))
    lt = jnp.transpose(logits)[0:N_EXPERTS, :]
    row = lax.broadcasted_iota(jnp.int32, lt.shape, 0)
    neg = jnp.float32(-jnp.inf)
    m1 = jnp.max(lt, axis=0, keepdims=True)
    i1 = jnp.min(jnp.where(lt == m1, row, N_EXPERTS), axis=0, keepdims=True)
    rest = jnp.where(row == i1, neg, lt)
    m2 = jnp.max(rest, axis=0, keepdims=True)
    i2 = jnp.min(jnp.where(rest == m2, row, N_EXPERTS), axis=0, keepdims=True)
    e2 = jnp.exp(m2 - m1)
    den = 1.0 + e2
    g1 = 1.0 / den
    g2 = e2 / den
    sel1 = row == i1
    sel2 = row == i2
    oh = jnp.where(sel1 | sel2, 1.0, 0.0)

    sp = lax.broadcasted_iota(jnp.int32, (ts, ts), 0)
    sc = lax.broadcasted_iota(jnp.int32, (ts, ts), 1)
    upper = jnp.where(sp < sc, 1.0, 0.0).astype(jnp.bfloat16)
    rank = _dot(_bf(oh), upper)
    rk1 = jnp.sum(jnp.where(sel1, rank, 0.0), axis=0, keepdims=True)
    rk2 = jnp.sum(jnp.where(sel2, rank, 0.0), axis=0, keepdims=True)

    cnts, offs = [], []
    off = jnp.int32(0)
    for e in range(N_EXPERTS):
        c = jnp.sum(oh[e:e + 1, :]).astype(jnp.int32)
        cnts.append(c)
        offs.append(off)
        off = off + c
    off1 = jnp.zeros_like(rk1)
    off2 = jnp.zeros_like(rk2)
    for e in range(N_EXPERTS):
        fe = offs[e].astype(jnp.float32)
        off1 = jnp.where(i1 == e, fe, off1)
        off2 = jnp.where(i2 == e, fe, off2)
    pos1 = (off1 + rk1).astype(jnp.int32)
    pos2 = (off2 + rk2).astype(jnp.int32)
    pos_ref[0] = jnp.concatenate([pos1, pos2, jnp.zeros((SUBLANES - TOP_K, ts), jnp.int32)], axis=0)
    gate_ref[0] = jnp.concatenate([g1, g2, jnp.zeros((SUBLANES - TOP_K, ts), jnp.float32)], axis=0)
    for e in range(N_EXPERTS):
        cnt_ref[t * N_EXPERTS + e] = cnts[e]


def _route_call(x2, ln_w, wr_hi, wr_lo):
    n, d = x2.shape
    ts = min(ROW_TILE, n)
    nt = n // ts
    return pl.pallas_call(
        _route_kernel,
        out_shape=(
            jax.ShapeDtypeStruct((nt, SUBLANES, ts), jnp.int32),
            jax.ShapeDtypeStruct((nt, SUBLANES, ts), jnp.float32),
            jax.ShapeDtypeStruct((nt * N_EXPERTS,), jnp.int32),
        ),
        grid=(nt,),
        in_specs=[
            pl.BlockSpec((ts, d), lambda i: (i, 0)),
            pl.BlockSpec((1, d), lambda i: (0, 0)),
            pl.BlockSpec((d, LANES), lambda i: (0, 0)),
            pl.BlockSpec((d, LANES), lambda i: (0, 0)),
        ],
        out_specs=(
            pl.BlockSpec((1, SUBLANES, ts), lambda i: (i, 0, 0)),
            pl.BlockSpec((1, SUBLANES, ts), lambda i: (i, 0, 0)),
            pl.BlockSpec(memory_space=pltpu.SMEM),
        ),
        compiler_params=pltpu.CompilerParams(dimension_semantics=("arbitrary",), vmem_limit_bytes=VMEM_LIMIT),
        name="route",
    )(x2, ln_w, wr_hi, wr_lo)


def _dispatch_kernel(seg_ref, cnt_ref, zstart_ref, zlen_ref, nval_ref, x_ref, ln_ref, pos_ref, xs_hbm,
                     stage_ref, zero_ref, sem, zsem, *, max_tiles):
    ts = x_ref.shape[0]
    n_sorted = TOP_K * ts
    t = pl.program_id(0)
    nt = pl.num_programs(0)
    h = _bf(_rms(x_ref[...], ln_ref[...]))
    pos = pos_ref[0]
    perm = _one_hot_rows(pos[0:1, :], pos[1:2, :], n_sorted)
    sorted_h = _dot(perm, h)

    @pl.when(t > 0)
    def _():
        pltpu.make_async_copy(stage_ref, xs_hbm.at[pl.ds(0, n_sorted * REC_ROWS)], sem).wait()

    _store_records(stage_ref, sorted_h)

    off = jnp.int32(0)
    for e in range(N_EXPERTS):
        c = cnt_ref[t * N_EXPERTS + e]
        _copy_records(stage_ref, off, xs_hbm, seg_ref[t * N_EXPERTS + e], c, sem, ts)
        off = off + c

    @pl.when(t == nt - 1)
    def _():
        pltpu.make_async_copy(stage_ref, xs_hbm.at[pl.ds(0, n_sorted * REC_ROWS)], sem).wait()
        tm = zero_ref.shape[0] // REC_ROWS
        zero_ref[...] = jnp.zeros_like(zero_ref)

        def tail_fill(k):
            return pltpu.make_async_copy(
                zero_ref, xs_hbm.at[pl.ds(pl.multiple_of((nval_ref[0] + k) * tm * REC_ROWS, REC_ROWS),
                                          tm * REC_ROWS)], zsem)

        for wait in (False, True):
            for e in range(N_EXPERTS):
                _copy_records(zero_ref, None, xs_hbm, zstart_ref[e], zlen_ref[e], zsem, tm, wait=wait)
            for k in range(N_EXPERTS):
                @pl.when(nval_ref[0] + k < max_tiles)
                def _(k=k, wait=wait):
                    tail_fill(k).wait() if wait else tail_fill(k).start()


def _dispatch_call(seg, cnt, zstart, zlen, n_valid, x2, ln_w, pos, max_tiles):
    n, d = x2.shape
    ts = min(ROW_TILE, n)
    nt = n // ts
    return pl.pallas_call(
        functools.partial(_dispatch_kernel, max_tiles=max_tiles),
        out_shape=jax.ShapeDtypeStruct((max_tiles * ROW_TILE * REC_ROWS, LANES), jnp.float32),
        grid_spec=pltpu.PrefetchScalarGridSpec(
            num_scalar_prefetch=5,
            grid=(nt,),
            in_specs=[
                pl.BlockSpec((ts, d), lambda i, *_: (i, 0)),
                pl.BlockSpec((1, d), lambda i, *_: (0, 0)),
                pl.BlockSpec((1, SUBLANES, ts), lambda i, *_: (i, 0, 0)),
            ],
            out_specs=pl.BlockSpec(memory_space=pl.ANY),
            scratch_shapes=[
                pltpu.VMEM((TOP_K * ts * REC_ROWS, LANES), jnp.float32),
                pltpu.VMEM((ROW_TILE * REC_ROWS, LANES), jnp.float32),
                pltpu.SemaphoreType.DMA(()),
                pltpu.SemaphoreType.DMA(()),
            ],
        ),
        compiler_params=pltpu.CompilerParams(dimension_semantics=("arbitrary",), vmem_limit_bytes=VMEM_LIMIT),
        name="dispatch",
    )(seg, cnt, zstart, zlen, n_valid, x2, ln_w, pos)


def _ffn_group_kernel(texp_ref, tnext_ref, ttail_ref, thead_ref, nval_ref, xs_ref, wg_hbm, wu_hbm, wd_hbm, ys_ref,
                      wg_ref, wu_ref, wd_ref, sg_ref, su_ref, sd_ref, sems):
    j = pl.program_id(0)
    n_chunks = D_FF // FF_CHUNK
    split = SWAP_SPLIT

    def chunk_copies(e, f, slot):
        c0 = f * FF_CHUNK
        return (pltpu.make_async_copy(wg_hbm.at[e, :, pl.ds(c0, FF_CHUNK)], sg_ref.at[slot], sems.at[slot]),
                pltpu.make_async_copy(wu_hbm.at[e, :, pl.ds(c0, FF_CHUNK)], su_ref.at[slot], sems.at[slot]),
                pltpu.make_async_copy(wd_hbm.at[e, pl.ds(c0, FF_CHUNK), :], sd_ref.at[slot], sems.at[slot]))

    def start(e, f):
        for cp in chunk_copies(e, f, f % 2):
            cp.start()

    def install(e, f):
        for cp in chunk_copies(e, f, f % 2):
            cp.wait()
        c0 = f * FF_CHUNK
        wg_ref[:, c0:c0 + FF_CHUNK] = _bf(sg_ref[f % 2])
        wu_ref[:, c0:c0 + FF_CHUNK] = _bf(su_ref[f % 2])
        wd_ref[c0:c0 + FF_CHUNK, :] = _bf(sd_ref[f % 2])

    def load_h():
        tm = xs_ref.shape[0] // REC_ROWS
        return jnp.concatenate([_bf(w) for w in _load_records(xs_ref, tm)], axis=1)

    @pl.when(j == 0)
    def _():
        e = texp_ref[0]
        start(e, 0)
        for f in range(n_chunks):
            if f + 1 < n_chunks:
                start(e, f + 1)
            install(e, f)

    valid = j < nval_ref[0]
    tail = ttail_ref[j] == 1
    head = thead_ref[j] == 1
    not_ = jnp.logical_not

    @pl.when(not_(valid))
    def _():
        ys_ref[...] = jnp.zeros_like(ys_ref)

    def finish_arrival(e):
        for f in range(split, n_chunks):
            install(e, f)
            if f + 2 < n_chunks:
                start(e, f + 2)

    def tile(is_tail, is_head):
        e_cur = texp_ref[j]
        e_next = tnext_ref[j]
        if is_tail and is_head:
            finish_arrival(e_cur)
        if is_tail:
            start(e_next, 0)
            start(e_next, 1)
        h = load_h()
        y = None
        for f in range(n_chunks):
            if is_head and not is_tail and f >= split:
                install(e_cur, f)
                if f + 2 < n_chunks:
                    start(e_cur, f + 2)
            part = _swiglu_chunk(h, wg_ref, wu_ref, wd_ref, f * FF_CHUNK)
            y = part if y is None else y + part
            if is_tail and f < split:
                install(e_next, f)
                if f + 2 < split:
                    start(e_next, f + 2)
        if is_tail:
            start(e_next, split)
            start(e_next, split + 1)
        _store_records(ys_ref, y)

    for is_tail in (False, True):
        for is_head in (False, True):
            cond = valid & (tail if is_tail else not_(tail)) & (head if is_head else not_(head))
            pl.when(cond)(functools.partial(tile, is_tail, is_head))


def _ffn_group_call(tile_expert, tile_next, tile_tail, tile_head, n_valid, xs, wg, wu, wd):
    rows = xs.shape[0]
    tm = ROW_TILE
    n_tiles = tile_expert.shape[0]
    d = wg.shape[1]
    return pl.pallas_call(
        _ffn_group_kernel,
        out_shape=jax.ShapeDtypeStruct((rows, LANES), jnp.float32),
        grid_spec=pltpu.PrefetchScalarGridSpec(
            num_scalar_prefetch=5,
            grid=(n_tiles,),
            in_specs=[
                pl.BlockSpec((tm * REC_ROWS, LANES), lambda j, *_: (j, 0)),
                pl.BlockSpec(memory_space=pl.ANY),
                pl.BlockSpec(memory_space=pl.ANY),
                pl.BlockSpec(memory_space=pl.ANY),
            ],
            out_specs=pl.BlockSpec((tm * REC_ROWS, LANES), lambda j, *_: (j, 0)),
            scratch_shapes=[
                pltpu.VMEM((d, D_FF), jnp.bfloat16),
                pltpu.VMEM((d, D_FF), jnp.bfloat16),
                pltpu.VMEM((D_FF, d), jnp.bfloat16),
                pltpu.VMEM((2, d, FF_CHUNK), jnp.float32),
                pltpu.VMEM((2, d, FF_CHUNK), jnp.float32),
                pltpu.VMEM((2, FF_CHUNK, d), jnp.float32),
                pltpu.SemaphoreType.DMA((2,)),
            ],
        ),
        compiler_params=pltpu.CompilerParams(dimension_semantics=("arbitrary",), vmem_limit_bytes=VMEM_LIMIT),
        name="ffn_group",
    )(tile_expert, tile_next, tile_tail, tile_head, n_valid, xs, wg, wu, wd)


def _combine_kernel(seg_ref, cnt_ref, x_ref, fin_ref, pos_ref, gate_ref, ys_hbm, o_ref, ybuf_ref, sems,
                    *, final_norm):
    ts = x_ref.shape[0]
    n_sorted = TOP_K * ts
    t = pl.program_id(0)
    nt = pl.num_programs(0)

    def fetch(tile, slot):
        off = jnp.int32(0)
        for e in range(N_EXPERTS):
            c = cnt_ref[tile * N_EXPERTS + e]
            _copy_records(ys_hbm, seg_ref[tile * N_EXPERTS + e], ybuf_ref.at[slot], off, c, sems.at[slot], ts)
            off = off + c

    @pl.when(t == 0)
    def _():
        fetch(0, 0)

    @pl.when(t + 1 < nt)
    def _():
        fetch(t + 1, (t + 1) % 2)

    slot = t % 2
    pltpu.make_async_copy(ys_hbm.at[pl.ds(0, n_sorted * REC_ROWS)], ybuf_ref.at[slot], sems.at[slot]).wait()
    ys = jnp.concatenate([_bf(w) for w in _load_records(ybuf_ref.at[slot], n_sorted)], axis=1)
    pos = pos_ref[0]
    gates = gate_ref[0]
    r = lax.broadcasted_iota(jnp.int32, (n_sorted, ts), 0)
    sel = _bf(jnp.where(r == pos[0:1, :], gates[0:1, :], 0.0) + jnp.where(r == pos[1:2, :], gates[1:2, :], 0.0))
    y = lax.dot_general(sel, ys, (((0,), (0,)), ((), ())), preferred_element_type=jnp.float32)
    out = x_ref[...] + y
    if final_norm:
        out = _rms(out, fin_ref[...])
    o_ref[...] = out


def _combine_call(seg, cnt, x2, fin_w, pos, gates, ys, *, final_norm):
    n, d = x2.shape
    ts = min(ROW_TILE, n)
    nt = n // ts
    return pl.pallas_call(
        functools.partial(_combine_kernel, final_norm=final_norm),
        out_shape=jax.ShapeDtypeStruct(x2.shape, x2.dtype),
        grid_spec=pltpu.PrefetchScalarGridSpec(
            num_scalar_prefetch=2,
            grid=(nt,),
            in_specs=[
                pl.BlockSpec((ts, d), lambda i, sg, ct: (i, 0)),
                pl.BlockSpec((1, d), lambda i, sg, ct: (0, 0)),
                pl.BlockSpec((1, SUBLANES, ts), lambda i, sg, ct: (i, 0, 0)),
                pl.BlockSpec((1, SUBLANES, ts), lambda i, sg, ct: (i, 0, 0)),
                pl.BlockSpec(memory_space=pl.ANY),
            ],
            out_specs=pl.BlockSpec((ts, d), lambda i, sg, ct: (i, 0)),
            scratch_shapes=[
                pltpu.VMEM((2, TOP_K * ts * REC_ROWS, LANES), jnp.float32),
                pltpu.SemaphoreType.DMA((2,)),
            ],
        ),
        compiler_params=pltpu.CompilerParams(dimension_semantics=("arbitrary",), vmem_limit_bytes=VMEM_LIMIT),
        name="combine",
    )(seg, cnt, x2, fin_w, pos, gates, ys)


def _moe(x2, ln_w, fin_w, w_router, wg, wu, wd, *, final_norm):
    n, d = x2.shape
    tm = ROW_TILE
    wr_p = jnp.concatenate([w_router, jnp.zeros((d, LANES - N_EXPERTS), w_router.dtype)], axis=1)
    wr_hi = _bf(wr_p)
    wr_lo = _bf(wr_p - wr_hi.astype(jnp.float32))
    pos, gates, cnt = _route_call(x2, ln_w, wr_hi, wr_lo)

    i32 = jnp.int32
    nt = cnt.shape[0] // N_EXPERTS
    cnt2 = cnt.reshape(nt, N_EXPERTS)
    totals = jnp.sum(cnt2, axis=0)
    tiles_e = (totals + tm - 1) // tm
    ends = jnp.cumsum(tiles_e)
    base = (ends - tiles_e) * tm
    seg = (base[None, :] + jnp.cumsum(cnt2, axis=0) - cnt2).reshape(-1).astype(i32)
    zstart = (base + totals).astype(i32)
    zlen = (tiles_e * tm - totals).astype(i32)
    n_valid = ends[-1].reshape(1).astype(i32)
    max_tiles = (TOP_K * n) // tm + N_EXPERTS
    j = jnp.minimum(jnp.arange(max_tiles, dtype=i32), n_valid - 1)
    tile_expert = jnp.sum((j[:, None] >= ends[None, :]).astype(i32), axis=1).astype(i32)
    tile_next = jnp.concatenate([tile_expert[1:], tile_expert[-1:]])
    tile_tail = (tile_next != tile_expert).astype(i32)
    tile_head = jnp.concatenate([jnp.zeros((1,), i32), tile_tail[:-1]])

    xs = _dispatch_call(seg, cnt, zstart, zlen, n_valid, x2, ln_w, pos, max_tiles)
    ys = _ffn_group_call(tile_expert, tile_next, tile_tail, tile_head, n_valid, xs, wg, wu, wd)
    return _combine_call(seg, cnt, x2, fin_w, pos, gates, ys, final_norm=final_norm)


def _prep_w_in(w):
    depth, d, _ = w.shape
    o_gr = 2 * GLA_KEY + 2 * GLA_VAL
    parts = [w[:, :, :o_gr], w[:, :, o_gr + GATE_RANK:], w[:, :, o_gr:o_gr + GATE_RANK],
             jnp.zeros((depth, d, GATE_PAD - GATE_RANK), w.dtype)]
    return _bf(jnp.concatenate(parts, axis=2))


def kernel(x, ln1_w, w_in, w_gate_up, b_gate, gla_norm_w, conv_w, conv_b, cn_w, cn_b, w_out, ln2_w, wd_gate,
           wd_up, wd_down, w_router, we_gate, we_up, we_down, final_norm_w):
    bsz, seq, d = x.shape
    depth = ln1_w.shape[0]
    n = bsz * seq
    row = lambda a: a.reshape(1, -1)
    rows = lambda a: a[:, None, :]
    w_in_p = _prep_w_in(w_in)
    wgu_p = _bf(jnp.concatenate(
        [w_gate_up, jnp.zeros((depth, GATE_PAD - GATE_RANK, GLA_KEY), w_gate_up.dtype)], axis=1))
    cw_p = jnp.concatenate([conv_w, jnp.zeros((depth, CONV_HALO - CONV_WIDTH, CONV_CH), conv_w.dtype)], axis=1)
    w_out_b = _bf(w_out)
    for l in range(depth):
        x = _mixer_call(x, l, rows(ln1_w), w_in_p, wgu_p, rows(b_gate), rows(gla_norm_w), cw_p, rows(conv_b),
                        rows(cn_w), rows(cn_b), w_out_b)
        x2 = x.reshape(n, d)
        last = l == depth - 1
        i = l // 2
        if l % 2 == 0:
            x2 = _ffn_dense_call(x2, row(ln2_w[l]), row(final_norm_w), _bf(wd_gate[i:i + 1]),
                                 _bf(wd_up[i:i + 1]), _bf(wd_down[i:i + 1]), final_norm=last)
        else:
            x2 = _moe(x2, row(ln2_w[l]), row(final_norm_w), w_router[i], we_gate[i], we_up[i], we_down[i],
                      final_norm=last)
        x = x2.reshape(bsz, seq, d)
    return x
```

```python
import functools

import jax
import jax.numpy as jnp
from jax import lax
from jax.experimental import pallas as pl
from jax.experimental.pallas import tpu as pltpu

D_MODEL = 1024
GLA_HEADS = 4
GLA_DK = 64
GLA_DV = 128
GLA_KEY = GLA_HEADS * GLA_DK
GLA_VAL = GLA_HEADS * GLA_DV
GATE_RANK = 16
GATE_NORMALIZER = 16.0
CONV_CH = 512
CONV_WIDTH = 31
D_FF = 2816
N_EXPERTS = 8
TOP_K = 2
EPS = 1e-6

LANES = 128
CHUNK = 64
TIME_TILE = 512
ROW_TILE = 512
FF_CHUNK = 256
SWAP_GROUP = 3
CONV_HALO = 32
GATE_PAD = LANES
IN_COLS_P = 2 * GLA_KEY + 2 * GLA_VAL + 2 * CONV_CH + GATE_PAD
VMEM_LIMIT = 56 * 1024 * 1024
SUBLANES = 8
REC_ROWS = D_MODEL // LANES

_OQ, _OK, _OV, _OG = 0, GLA_KEY, 2 * GLA_KEY, 2 * GLA_KEY + GLA_VAL
_OCA = _OG + GLA_VAL
_OCB = _OCA + CONV_CH
_OGR = _OCB + CONV_CH

_LEVELS = (32, 16, 8, 4, 2, 1)


def _rms(x, w):
    return x * lax.rsqrt(jnp.mean(x * x, axis=-1, keepdims=True) + EPS) * w


def _bf(x):
    return x.astype(jnp.bfloat16)


def _dot(a, b):
    return jnp.dot(a, b, preferred_element_type=jnp.float32)


def _boundary(b, hs, rows):
    n, c = b.shape
    blk = 2 * hs
    if blk >= 8:
        b3 = b.reshape(n // blk, blk, c)
        return jnp.broadcast_to(b3[:, hs - 1:hs, :], (n // blk, blk, c)).reshape(n, c)
    y = pltpu.roll(b, n - (hs - 1), 0) if hs > 1 else b
    s = 1
    while s < blk:
        y = jnp.where((rows & s) != 0, pltpu.roll(y, s, 0), y)
        s *= 2
    return y


def _mixer_kernel(x_ref, ln_ref, win_ref, wgu_ref, bg_ref, gnw_ref, cw_ref, cb_ref, cnw_ref, cnb_ref,
                  wout_ref, o_ref, s_ref, ubuf_ref, ush_ref, y_ref):
    tt = x_ref.shape[1]
    nch = tt // CHUNK
    t = pl.program_id(1)

    @pl.when(t == 0)
    def _():
        s_ref[...] = jnp.zeros_like(s_ref)
        ubuf_ref[0:CONV_HALO, :] = jnp.zeros((CONV_HALO, CONV_CH), jnp.float32)

    x = x_ref[0]
    h = _bf(_rms(x, ln_ref[...]))
    z = _dot(h, win_ref[...])

    q = z[:, _OQ:_OQ + GLA_KEY] * (GLA_DK ** -0.5)
    k = z[:, _OK:_OK + GLA_KEY]
    v = z[:, _OV:_OV + GLA_VAL]
    g = z[:, _OG:_OG + GLA_VAL]
    ca = z[:, _OCA:_OCA + CONV_CH]
    cb = z[:, _OCB:_OCB + CONV_CH]
    gr = z[:, _OGR:_OGR + GATE_PAD]

    ubuf_ref[CONV_HALO:CONV_HALO + tt, :] = ca * jax.nn.sigmoid(cb)
    off0 = CONV_HALO - (CONV_WIDTH - 1)
    ufull = ubuf_ref[...]
    for r in range(1, SUBLANES):
        ush_ref[r - 1] = pltpu.roll(ufull, CONV_HALO + tt - r, 0)[0:CONV_HALO + tt - SUBLANES, :]
    for r0 in range(0, tt, CHUNK):
        for c0 in range(0, CONV_CH, LANES):
            acc = jnp.broadcast_to(cb_ref[:, c0:c0 + LANES], (CHUNK, LANES))
            for j in range(CONV_WIDTH):
                r = (off0 + j) % SUBLANES
                a0 = r0 + off0 + j - r
                src = ubuf_ref if r == 0 else ush_ref.at[r - 1]
                acc = acc + cw_ref[j:j + 1, c0:c0 + LANES] * src[a0:a0 + CHUNK, c0:c0 + LANES]
            y_ref[r0:r0 + CHUNK, c0:c0 + LANES] = acc
    ubuf_ref[0:CONV_HALO, :] = ubuf_ref[tt:tt + CONV_HALO, :]
    yc = y_ref[...]
    mu = jnp.mean(yc, axis=-1, keepdims=True)
    yd = yc - mu
    var = jnp.mean(yd * yd, axis=-1, keepdims=True)
    u = yd * lax.rsqrt(var + EPS) * cnw_ref[...] + cnb_ref[...]
    u = u * jax.nn.sigmoid(u)

    logit = _dot(_bf(gr), wgu_ref[...]) + bg_ref[...]
    la = jax.nn.log_sigmoid(logit) * (1.0 / GATE_NORMALIZER)
    rows = lax.broadcasted_iota(jnp.int32, (tt, GLA_KEY), 0)
    rc = rows & (CHUNK - 1)
    b = la
    s = 1
    while s < CHUNK:
        b = b + jnp.where(rc >= s, pltpu.roll(b, s, 0), 0.0)
        s *= 2
    b3 = b.reshape(nch, CHUNK, GLA_KEY)
    blast3 = b3[:, CHUNK - 1:CHUNK, :]
    blast = jnp.broadcast_to(blast3, (nch, CHUNK, GLA_KEY)).reshape(tt, GLA_KEY)
    qe = _bf(q * jnp.exp(b))
    kl = _bf(k * jnp.exp(blast - b))
    vb = _bf(v)
    dl = jnp.exp(blast3.reshape(nch, GLA_KEY))
    dl_t = jnp.transpose(jnp.concatenate([dl, jnp.zeros((LANES - nch, GLA_KEY), jnp.float32)], axis=0))

    qh, kh = [_bf(q)], [_bf(k)]
    for hs in _LEVELS:
        e = jnp.exp(-jnp.abs(b - _boundary(b, hs, rows)))
        qh.append(_bf(q * e))
        kh.append(_bf(k * e))

    ii = lax.broadcasted_iota(jnp.int32, (CHUNK, GLA_HEADS * CHUNK), 0)
    jj = lax.broadcasted_iota(jnp.int32, (CHUNK, GLA_HEADS * CHUNK), 1) & (CHUNK - 1)
    masks = [ii == jj]
    for hs in _LEVELS:
        blk = 2 * hs
        masks.append(((ii // blk) == (jj // blk)) & ((ii & (blk - 1)) >= hs) & ((jj & (blk - 1)) < hs))
    rk = lax.broadcasted_iota(jnp.int32, (GLA_HEADS * CHUNK, GLA_KEY), 0) // CHUNK
    ck = lax.broadcasted_iota(jnp.int32, (GLA_HEADS * CHUNK, GLA_KEY), 1) // GLA_DK
    bd_k = (rk == ck).astype(jnp.bfloat16)
    rv = lax.broadcasted_iota(jnp.int32, (GLA_HEADS * CHUNK, GLA_VAL), 0) // CHUNK
    cv = lax.broadcasted_iota(jnp.int32, (GLA_HEADS * CHUNK, GLA_VAL), 1) // GLA_DV
    bd_v = rv == cv
    bd_vb = bd_v.astype(jnp.bfloat16)

    st = s_ref[...]
    o_parts = []
    for c in range(nch):
        sl = slice(c * CHUNK, (c + 1) * CHUNK)
        att = jnp.zeros((CHUNK, GLA_HEADS * CHUNK), jnp.float32)
        for lvl in range(len(masks)):
            kbd = jnp.concatenate([kh[lvl][sl]] * GLA_HEADS, axis=0) * bd_k
            sc = lax.dot_general(qh[lvl][sl], kbd, (((1,), (1,)), ((), ())),
                                 preferred_element_type=jnp.float32)
            att = jnp.where(masks[lvl], sc, att)
        vbd = jnp.concatenate([vb[sl]] * GLA_HEADS, axis=0) * bd_vb
        o_parts.append(_dot(_bf(att), vbd) + _dot(qe[sl], _bf(st)))
        upd = lax.dot_general(kl[sl], vb[sl], (((0,), (0,)), ((), ())), preferred_element_type=jnp.float32)
        st = st * dl_t[:, c:c + 1] + jnp.where(bd_v, upd, 0.0)
    s_ref[...] = st
    o = jnp.concatenate(o_parts, axis=0)

    gnw = gnw_ref[...]
    heads = []
    for hd in range(GLA_HEADS):
        oh = o[:, hd * GLA_DV:(hd + 1) * GLA_DV]
        heads.append(_rms(oh, gnw))
    o = jnp.concatenate(heads, axis=-1) * (g * jax.nn.sigmoid(g))

    mix = _bf(jnp.concatenate([o, u], axis=-1))
    o_ref[0] = x + _dot(mix, wout_ref[...])


def _mixer_call(x, layer, ln_w, w_in_p, wgu_p, b_gate, gnw, conv_w, conv_b, cn_w, cn_b, w_out_b):
    bsz, seq, d = x.shape
    tt = min(TIME_TILE, seq)
    assert seq % tt == 0 and tt % CHUNK == 0
    const = lambda *shape: pl.BlockSpec((None,) + shape, lambda b, t: (layer,) + (0,) * len(shape))
    return pl.pallas_call(
        _mixer_kernel,
        out_shape=jax.ShapeDtypeStruct(x.shape, x.dtype),
        grid=(bsz, seq // tt),
        in_specs=[
            pl.BlockSpec((1, tt, d), lambda b, t: (b, t, 0)),
            const(1, d), const(d, IN_COLS_P), const(GATE_PAD, GLA_KEY), const(1, GLA_KEY), const(1, GLA_DV),
            const(CONV_HALO, CONV_CH), const(1, CONV_CH), const(1, CONV_CH), const(1, CONV_CH),
            const(GLA_VAL + CONV_CH, d),
        ],
        out_specs=pl.BlockSpec((1, tt, d), lambda b, t: (b, t, 0)),
        scratch_shapes=[
            pltpu.VMEM((GLA_HEADS * GLA_DK, GLA_VAL), jnp.float32),
            pltpu.VMEM((CONV_HALO + tt, CONV_CH), jnp.float32),
            pltpu.VMEM((SUBLANES - 1, CONV_HALO + tt - SUBLANES, CONV_CH), jnp.float32),
            pltpu.VMEM((tt, CONV_CH), jnp.float32),
        ],
        compiler_params=pltpu.CompilerParams(dimension_semantics=("arbitrary", "arbitrary"),
                                             vmem_limit_bytes=VMEM_LIMIT),
        name="mixer",
    )(x, ln_w, w_in_p, wgu_p, b_gate, gnw, conv_w, conv_b, cn_w, cn_b, w_out_b)


def _swiglu_chunk(h, wg_ref, wu_ref, wd_ref, f0):
    gt = _dot(h, wg_ref[:, f0:f0 + FF_CHUNK])
    up = _dot(h, wu_ref[:, f0:f0 + FF_CHUNK])
    a = _bf(gt * jax.nn.sigmoid(gt) * up)
    return _dot(a, wd_ref[f0:f0 + FF_CHUNK, :])


def _swiglu(h, wg_ref, wu_ref, wd_ref):
    y = None
    for f0 in range(0, D_FF, FF_CHUNK):
        part = _swiglu_chunk(h, wg_ref, wu_ref, wd_ref, f0)
        y = part if y is None else y + part
    return y


def _ffn_dense_kernel(x_ref, ln_ref, fin_ref, wg_ref, wu_ref, wd_ref, o_ref, *, final_norm):
    x = x_ref[...]
    out = x + _swiglu(_bf(_rms(x, ln_ref[...])), wg_ref.at[0], wu_ref.at[0], wd_ref.at[0])
    if final_norm:
        out = _rms(out, fin_ref[...])
    o_ref[...] = out


def _ffn_dense_call(x2, ln_w, fin_w, wg, wu, wd, *, final_norm):
    n, d = x2.shape
    tm = min(ROW_TILE, n)
    assert n % tm == 0
    return pl.pallas_call(
        functools.partial(_ffn_dense_kernel, final_norm=final_norm),
        out_shape=jax.ShapeDtypeStruct(x2.shape, x2.dtype),
        grid=(n // tm,),
        in_specs=[
            pl.BlockSpec((tm, d), lambda i: (i, 0)),
            pl.BlockSpec((1, d), lambda i: (0, 0)),
            pl.BlockSpec((1, d), lambda i: (0, 0)),
            pl.BlockSpec((1, d, D_FF), lambda i: (0, 0, 0)),
            pl.BlockSpec((1, d, D_FF), lambda i: (0, 0, 0)),
            pl.BlockSpec((1, D_FF, d), lambda i: (0, 0, 0)),
        ],
        out_specs=pl.BlockSpec((tm, d), lambda i: (i, 0)),
        compiler_params=pltpu.CompilerParams(dimension_semantics=("arbitrary",), vmem_limit_bytes=VMEM_LIMIT),
        name="ffn_dense",
    )(x2, ln_w, fin_w, wg, wu, wd)


def _store_records(rec_ref, vals):
    m = vals.shape[0]
    for s in range(REC_ROWS):
        rec_ref[pl.ds(s, m, stride=REC_ROWS), :] = vals[:, s * LANES:(s + 1) * LANES]


def _load_records(rec_ref, m):
    return [rec_ref[pl.ds(s, m, stride=REC_ROWS), :] for s in range(REC_ROWS)]


def _copy_records(src_ref, src_off, dst_ref, dst_off, n, sem, max_rows, wait=False):
    bit = max_rows.bit_length() - 1
    while bit >= 0:
        size = (1 << bit) * REC_ROWS
        done = lax.shift_left(lax.shift_right_logical(n, bit + 1), bit + 1)
        src0 = 0 if src_off is None else pl.multiple_of((src_off + done) * REC_ROWS, REC_ROWS)
        dst0 = pl.multiple_of((dst_off + done) * REC_ROWS, REC_ROWS)

        @pl.when((lax.shift_right_logical(n, bit) & 1) == 1)
        def _(size=size, src0=src0, dst0=dst0):
            cp = pltpu.make_async_copy(src_ref.at[pl.ds(src0, size)], dst_ref.at[pl.ds(dst0, size)], sem)
            cp.wait() if wait else cp.start()
        bit -= 1


def _one_hot_rows(pos0, pos1, n_rows):
    r = lax.broadcasted_iota(jnp.int32, (n_rows, pos0.shape[1]), 0)
    return jnp.where((r == pos0) | (r == pos1), 1.0, 0.0).astype(jnp.bfloat16)


def _route_kernel(x_ref, ln_ref, wrh_ref, wrl_ref, pos_ref, gate_ref, cnt_ref):
    ts = x_ref.shape[0]
    t = pl.program_id(0)
    h = _rms(x_ref[...], ln_ref[...])
    h_hi = _bf(h)
    h_lo = _bf(h - h_hi.astype(jnp.float32))
    logits = _dot(h_hi, wrh_ref[...]) + (_dot(h_hi, wrl_ref[...]) + _dot(h_lo, wrh_ref[...]))
    lt = jnp.transpose(logits)[0:N_EXPERTS, :]
    row = lax.broadcasted_iota(jnp.int32, lt.shape, 0)
    neg = jnp.float32(-jnp.inf)
    m1 = jnp.max(lt, axis=0, keepdims=True)
    i1 = jnp.min(jnp.where(lt == m1, row, N_EXPERTS), axis=0, keepdims=True)
    rest = jnp.where(row == i1, neg, lt)
    m2 = jnp.max(rest, axis=0, keepdims=True)
    i2 = jnp.min(jnp.where(rest == m2, row, N_EXPERTS), axis=0, keepdims=True)
    e2 = jnp.exp(m2 - m1)
    den = 1.0 + e2
    g1 = 1.0 / den
    g2 = e2 / den
    sel1 = row == i1
    sel2 = row == i2
    oh = jnp.where(sel1 | sel2, 1.0, 0.0)

    sp = lax.broadcasted_iota(jnp.int32, (ts, ts), 0)
    sc = lax.broadcasted_iota(jnp.int32, (ts, ts), 1)
    upper = jnp.where(sp < sc, 1.0, 0.0).astype(jnp.bfloat16)
    rank = _dot(_bf(oh), upper)
    rk1 = jnp.sum(jnp.where(sel1, rank, 0.0), axis=0, keepdims=True)
    rk2 = jnp.sum(jnp.where(sel2, rank, 0.0), axis=0, keepdims=True)

    cnts, offs = [], []
    off = jnp.int32(0)
    for e in range(N_EXPERTS):
        c = jnp.sum(oh[e:e + 1, :]).astype(jnp.int32)
        cnts.append(c)
        offs.append(off)
        off = off + c
    off1 = jnp.zeros_like(rk1)
    off2 = jnp.zeros_like(rk2)
    for e in range(N_EXPERTS):
        fe = offs[e].astype(jnp.float32)
        off1 = jnp.where(i1 == e, fe, off1)
        off2 = jnp.where(i2 == e, fe, off2)
    pos1 = (off1 + rk1).astype(jnp.int32)
    pos2 = (off2 + rk2).astype(jnp.int32)
    pos_ref[0] = jnp.concatenate([pos1, pos2, jnp.zeros((SUBLANES - TOP_K, ts), jnp.int32)], axis=0)
    gate_ref[0] = jnp.concatenate([g1, g2, jnp.zeros((SUBLANES - TOP_K, ts), jnp.float32)], axis=0)
    for e in range(N_EXPERTS):
        cnt_ref[t * N_EXPERTS + e] = cnts[e]


def _route_call(x2, ln_w, wr_hi, wr_lo):
    n, d = x2.shape
    ts = min(ROW_TILE, n)
    nt = n // ts
    return pl.pallas_call(
        _route_kernel,
        out_shape=(
            jax.ShapeDtypeStruct((nt, SUBLANES, ts), jnp.int32),
            jax.ShapeDtypeStruct((nt, SUBLANES, ts), jnp.float32),
            jax.ShapeDtypeStruct((nt * N_EXPERTS,), jnp.int32),
        ),
        grid=(nt,),
        in_specs=[
            pl.BlockSpec((ts, d), lambda i: (i, 0)),
            pl.BlockSpec((1, d), lambda i: (0, 0)),
            pl.BlockSpec((d, LANES), lambda i: (0, 0)),
            pl.BlockSpec((d, LANES), lambda i: (0, 0)),
        ],
        out_specs=(
            pl.BlockSpec((1, SUBLANES, ts), lambda i: (i, 0, 0)),
            pl.BlockSpec((1, SUBLANES, ts), lambda i: (i, 0, 0)),
            pl.BlockSpec(memory_space=pltpu.SMEM),
        ),
        compiler_params=pltpu.CompilerParams(dimension_semantics=("arbitrary",), vmem_limit_bytes=VMEM_LIMIT),
        name="route",
    )(x2, ln_w, wr_hi, wr_lo)


def _dispatch_kernel(seg_ref, cnt_ref, zstart_ref, zlen_ref, nval_ref, x_ref, ln_ref, pos_ref, xs_hbm,
                     stage_ref, zero_ref, sem, zsem, *, max_tiles):
    ts = x_ref.shape[0]
    n_sorted = TOP_K * ts
    t = pl.program_id(0)
    nt = pl.num_programs(0)
    h = _bf(_rms(x_ref[...], ln_ref[...]))
    pos = pos_ref[0]
    perm = _one_hot_rows(pos[0:1, :], pos[1:2, :], n_sorted)
    sorted_h = _dot(perm, h)

    @pl.when(t > 0)
    def _():
        pltpu.make_async_copy(stage_ref, xs_hbm.at[pl.ds(0, n_sorted * REC_ROWS)], sem).wait()

    _store_records(stage_ref, sorted_h)

    off = jnp.int32(0)
    for e in range(N_EXPERTS):
        c = cnt_ref[t * N_EXPERTS + e]
        _copy_records(stage_ref, off, xs_hbm, seg_ref[t * N_EXPERTS + e], c, sem, ts)
        off = off + c

    @pl.when(t == nt - 1)
    def _():
        pltpu.make_async_copy(stage_ref, xs_hbm.at[pl.ds(0, n_sorted * REC_ROWS)], sem).wait()
        tm = zero_ref.shape[0] // REC_ROWS
        zero_ref[...] = jnp.zeros_like(zero_ref)

        def tail_fill(k):
            return pltpu.make_async_copy(
                zero_ref, xs_hbm.at[pl.ds(pl.multiple_of((nval_ref[0] + k) * tm * REC_ROWS, REC_ROWS),
                                          tm * REC_ROWS)], zsem)

        for wait in (False, True):
            for e in range(N_EXPERTS):
                _copy_records(zero_ref, None, xs_hbm, zstart_ref[e], zlen_ref[e], zsem, tm, wait=wait)
            for k in range(N_EXPERTS):
                @pl.when(nval_ref[0] + k < max_tiles)
                def _(k=k, wait=wait):
                    tail_fill(k).wait() if wait else tail_fill(k).start()


def _dispatch_call(seg, cnt, zstart, zlen, n_valid, x2, ln_w, pos, max_tiles):
    n, d = x2.shape
    ts = min(ROW_TILE, n)
    nt = n // ts
    return pl.pallas_call(
        functools.partial(_dispatch_kernel, max_tiles=max_tiles),
        out_shape=jax.ShapeDtypeStruct((max_tiles * ROW_TILE * REC_ROWS, LANES), jnp.float32),
        grid_spec=pltpu.PrefetchScalarGridSpec(
            num_scalar_prefetch=5,
            grid=(nt,),
            in_specs=[
                pl.BlockSpec((ts, d), lambda i, *_: (i, 0)),
                pl.BlockSpec((1, d), lambda i, *_: (0, 0)),
                pl.BlockSpec((1, SUBLANES, ts), lambda i, *_: (i, 0, 0)),
            ],
            out_specs=pl.BlockSpec(memory_space=pl.ANY),
            scratch_shapes=[
                pltpu.VMEM((TOP_K * ts * REC_ROWS, LANES), jnp.float32),
                pltpu.VMEM((ROW_TILE * REC_ROWS, LANES), jnp.float32),
                pltpu.SemaphoreType.DMA(()),
                pltpu.SemaphoreType.DMA(()),
            ],
        ),
        compiler_params=pltpu.CompilerParams(dimension_semantics=("arbitrary",), vmem_limit_bytes=VMEM_LIMIT),
        name="dispatch",
    )(seg, cnt, zstart, zlen, n_valid, x2, ln_w, pos)


def _ffn_group_kernel(texp_ref, tnext_ref, tswap_ref, nval_ref, xs_ref, wg_hbm, wu_hbm, wd_hbm, ys_ref,
                      wg_ref, wu_ref, wd_ref, sg_ref, su_ref, sd_ref, sems):
    j = pl.program_id(0)
    n_chunks = D_FF // FF_CHUNK
    gw = SWAP_GROUP * FF_CHUNK
    groups = [(c0, min(c0 + gw, D_FF)) for c0 in range(0, D_FF, gw)]

    def group_copies(e, g):
        c0, c1 = groups[g]
        w, slot = c1 - c0, g % 2
        return (pltpu.make_async_copy(wg_hbm.at[e, :, pl.ds(c0, w)], sg_ref.at[slot, :, pl.ds(0, w)],
                                      sems.at[slot]),
                pltpu.make_async_copy(wu_hbm.at[e, :, pl.ds(c0, w)], su_ref.at[slot, :, pl.ds(0, w)],
                                      sems.at[slot]),
                pltpu.make_async_copy(wd_hbm.at[e, pl.ds(c0, w), :], sd_ref.at[slot, pl.ds(0, w), :],
                                      sems.at[slot]))

    def start(e, g):
        for cp in group_copies(e, g):
            cp.start()

    def install(e, g):
        for cp in group_copies(e, g):
            cp.wait()
        c0, c1 = groups[g]
        w, slot = c1 - c0, g % 2
        wg_ref[:, c0:c1] = _bf(sg_ref[slot, :, 0:w])
        wu_ref[:, c0:c1] = _bf(su_ref[slot, :, 0:w])
        wd_ref[c0:c1, :] = _bf(sd_ref[slot, 0:w, :])

    def load_h():
        tm = xs_ref.shape[0] // REC_ROWS
        return jnp.concatenate([_bf(w) for w in _load_records(xs_ref, tm)], axis=1)

    @pl.when(j == 0)
    def _():
        e = texp_ref[0]
        start(e, 0)
        for g in range(len(groups)):
            if g + 1 < len(groups):
                start(e, g + 1)
            install(e, g)

    valid = j < nval_ref[0]
    swap = tswap_ref[j] == 1
    not_ = jnp.logical_not

    @pl.when(not_(valid))
    def _():
        ys_ref[...] = jnp.zeros_like(ys_ref)

    @pl.when(valid & not_(swap))
    def _():
        _store_records(ys_ref, _swiglu(load_h(), wg_ref, wu_ref, wd_ref))

    @pl.when(valid & swap)
    def _():
        e = tnext_ref[j]
        start(e, 0)
        start(e, 1)
        h = load_h()
        y = None
        for f in range(n_chunks):
            part = _swiglu_chunk(h, wg_ref, wu_ref, wd_ref, f * FF_CHUNK)
            y = part if y is None else y + part
            g = f // SWAP_GROUP
            if (f + 1) * FF_CHUNK == groups[g][1]:
                install(e, g)
                if g + 2 < len(groups):
                    start(e, g + 2)
        _store_records(ys_ref, y)


def _ffn_group_call(tile_expert, tile_next, tile_swap, n_valid, xs, wg, wu, wd):
    rows = xs.shape[0]
    tm = ROW_TILE
    n_tiles = tile_expert.shape[0]
    d = wg.shape[1]
    return pl.pallas_call(
        _ffn_group_kernel,
        out_shape=jax.ShapeDtypeStruct((rows, LANES), jnp.float32),
        grid_spec=pltpu.PrefetchScalarGridSpec(
            num_scalar_prefetch=4,
            grid=(n_tiles,),
            in_specs=[
                pl.BlockSpec((tm * REC_ROWS, LANES), lambda j, *_: (j, 0)),
                pl.BlockSpec(memory_space=pl.ANY),
                pl.BlockSpec(memory_space=pl.ANY),
                pl.BlockSpec(memory_space=pl.ANY),
            ],
            out_specs=pl.BlockSpec((tm * REC_ROWS, LANES), lambda j, *_: (j, 0)),
            scratch_shapes=[
                pltpu.VMEM((d, D_FF), jnp.bfloat16),
                pltpu.VMEM((d, D_FF), jnp.bfloat16),
                pltpu.VMEM((D_FF, d), jnp.bfloat16),
                pltpu.VMEM((2, d, SWAP_GROUP * FF_CHUNK), jnp.float32),
                pltpu.VMEM((2, d, SWAP_GROUP * FF_CHUNK), jnp.float32),
                pltpu.VMEM((2, SWAP_GROUP * FF_CHUNK, d), jnp.float32),
                pltpu.SemaphoreType.DMA((2,)),
            ],
        ),
        compiler_params=pltpu.CompilerParams(dimension_semantics=("arbitrary",), vmem_limit_bytes=VMEM_LIMIT),
        name="ffn_group",
    )(tile_expert, tile_next, tile_swap, n_valid, xs, wg, wu, wd)


def _combine_kernel(seg_ref, cnt_ref, x_ref, fin_ref, pos_ref, gate_ref, ys_hbm, o_ref, ybuf_ref, sems,
                    *, final_norm):
    ts = x_ref.shape[0]
    n_sorted = TOP_K * ts
    t = pl.program_id(0)
    nt = pl.num_programs(0)

    def fetch(tile, slot):
        off = jnp.int32(0)
        for e in range(N_EXPERTS):
            c = cnt_ref[tile * N_EXPERTS + e]
            _copy_records(ys_hbm, seg_ref[tile * N_EXPERTS + e], ybuf_ref.at[slot], off, c, sems.at[slot], ts)
            off = off + c

    @pl.when(t == 0)
    def _():
        fetch(0, 0)

    @pl.when(t + 1 < nt)
    def _():
        fetch(t + 1, (t + 1) % 2)

    slot = t % 2
    pltpu.make_async_copy(ys_hbm.at[pl.ds(0, n_sorted * REC_ROWS)], ybuf_ref.at[slot], sems.at[slot]).wait()
    ys = jnp.concatenate([_bf(w) for w in _load_records(ybuf_ref.at[slot], n_sorted)], axis=1)
    pos = pos_ref[0]
    gates = gate_ref[0]
    r = lax.broadcasted_iota(jnp.int32, (n_sorted, ts), 0)
    sel = _bf(jnp.where(r == pos[0:1, :], gates[0:1, :], 0.0) + jnp.where(r == pos[1:2, :], gates[1:2, :], 0.0))
    y = lax.dot_general(sel, ys, (((0,), (0,)), ((), ())), preferred_element_type=jnp.float32)
    out = x_ref[...] + y
    if final_norm:
        out = _rms(out, fin_ref[...])
    o_ref[...] = out


def _combine_call(seg, cnt, x2, fin_w, pos, gates, ys, *, final_norm):
    n, d = x2.shape
    ts = min(ROW_TILE, n)
    nt = n // ts
    return pl.pallas_call(
        functools.partial(_combine_kernel, final_norm=final_norm),
        out_shape=jax.ShapeDtypeStruct(x2.shape, x2.dtype),
        grid_spec=pltpu.PrefetchScalarGridSpec(
            num_scalar_prefetch=2,
            grid=(nt,),
            in_specs=[
                pl.BlockSpec((ts, d), lambda i, sg, ct: (i, 0)),
                pl.BlockSpec((1, d), lambda i, sg, ct: (0, 0)),
                pl.BlockSpec((1, SUBLANES, ts), lambda i, sg, ct: (i, 0, 0)),
                pl.BlockSpec((1, SUBLANES, ts), lambda i, sg, ct: (i, 0, 0)),
                pl.BlockSpec(memory_space=pl.ANY),
            ],
            out_specs=pl.BlockSpec((ts, d), lambda i, sg, ct: (i, 0)),
            scratch_shapes=[
                pltpu.VMEM((2, TOP_K * ts * REC_ROWS, LANES), jnp.float32),
                pltpu.SemaphoreType.DMA((2,)),
            ],
        ),
        compiler_params=pltpu.CompilerParams(dimension_semantics=("arbitrary",), vmem_limit_bytes=VMEM_LIMIT),
        name="combine",
    )(seg, cnt, x2, fin_w, pos, gates, ys)


def _moe(x2, ln_w, fin_w, w_router, wg, wu, wd, *, final_norm):
    n, d = x2.shape
    tm = ROW_TILE
    wr_p = jnp.concatenate([w_router, jnp.zeros((d, LANES - N_EXPERTS), w_router.dtype)], axis=1)
    wr_hi = _bf(wr_p)
    wr_lo = _bf(wr_p - wr_hi.astype(jnp.float32))
    pos, gates, cnt = _route_call(x2, ln_w, wr_hi, wr_lo)

    i32 = jnp.int32
    nt = cnt.shape[0] // N_EXPERTS
    cnt2 = cnt.reshape(nt, N_EXPERTS)
    totals = jnp.sum(cnt2, axis=0)
    tiles_e = (totals + tm - 1) // tm
    ends = jnp.cumsum(tiles_e)
    base = (ends - tiles_e) * tm
    seg = (base[None, :] + jnp.cumsum(cnt2, axis=0) - cnt2).reshape(-1).astype(i32)
    zstart = (base + totals).astype(i32)
    zlen = (tiles_e * tm - totals).astype(i32)
    n_valid = ends[-1].reshape(1).astype(i32)
    max_tiles = (TOP_K * n) // tm + N_EXPERTS
    j = jnp.minimum(jnp.arange(max_tiles, dtype=i32), n_valid - 1)
    tile_expert = jnp.sum((j[:, None] >= ends[None, :]).astype(i32), axis=1).astype(i32)
    tile_next = jnp.concatenate([tile_expert[1:], tile_expert[-1:]])
    tile_swap = (tile_next != tile_expert).astype(i32)

    xs = _dispatch_call(seg, cnt, zstart, zlen, n_valid, x2, ln_w, pos, max_tiles)
    ys = _ffn_group_call(tile_expert, tile_next, tile_swap, n_valid, xs, wg, wu, wd)
    return _combine_call(seg, cnt, x2, fin_w, pos, gates, ys, final_norm=final_norm)


def _prep_w_in(w):
    depth, d, _ = w.shape
    o_gr = 2 * GLA_KEY + 2 * GLA_VAL
    parts = [w[:, :, :o_gr], w[:, :, o_gr + GATE_RANK:], w[:, :, o_gr:o_gr + GATE_RANK],
             jnp.zeros((depth, d, GATE_PAD - GATE_RANK), w.dtype)]
    return _bf(jnp.concatenate(parts, axis=2))


def kernel(x, ln1_w, w_in, w_gate_up, b_gate, gla_norm_w, conv_w, conv_b, cn_w, cn_b, w_out, ln2_w, wd_gate,
           wd_up, wd_down, w_router, we_gate, we_up, we_down, final_norm_w):
    bsz, seq, d = x.shape
    depth = ln1_w.shape[0]
    n = bsz * seq
    row = lambda a: a.reshape(1, -1)
    rows = lambda a: a[:, None, :]
    w_in_p = _prep_w_in(w_in)
    wgu_p = _bf(jnp.concatenate(
        [w_gate_up, jnp.zeros((depth, GATE_PAD - GATE_RANK, GLA_KEY), w_gate_up.dtype)], axis=1))
    cw_p = jnp.concatenate([conv_w, jnp.zeros((depth, CONV_HALO - CONV_WIDTH, CONV_CH), conv_w.dtype)], axis=1)
    w_out_b = _bf(w_out)
    for l in range(depth):
        x = _mixer_call(x, l, rows(ln1_w), w_in_p, wgu_p, rows(b_gate), rows(gla_norm_w), cw_p, rows(conv_b),
                        rows(cn_w), rows(cn_b), w_out_b)
        x2 = x.reshape(n, d)
        last = l == depth - 1
        i = l // 2
        if l % 2 == 0:
            x2 = _ffn_dense_call(x2, row(ln2_w[l]), row(final_norm_w), _bf(wd_gate[i:i + 1]),
                                 _bf(wd_up[i:i + 1]), _bf(wd_down[i:i + 1]), final_norm=last)
        else:
            x2 = _moe(x2, row(ln2_w[l]), row(final_norm_w), w_router[i], we_gate[i], we_up[i], we_down[i],
                      final_norm=last)
        x = x2.reshape(bsz, seq, d)
    return x
```

```python
import functools

import jax
import jax.numpy as jnp
from jax import lax
from jax.experimental import pallas as pl
from jax.experimental.pallas import tpu as pltpu

D_MODEL = 1024
GLA_HEADS = 4
GLA_DK = 64
GLA_DV = 128
GLA_KEY = GLA_HEADS * GLA_DK
GLA_VAL = GLA_HEADS * GLA_DV
GATE_RANK = 16
GATE_NORMALIZER = 16.0
CONV_CH = 512
CONV_WIDTH = 31
D_FF = 2816
N_EXPERTS = 8
TOP_K = 2
EPS = 1e-6

LANES = 128
CHUNK = 64
TIME_TILE = 512
ROW_TILE = 512
FF_CHUNK = 256
SWAP_GROUP = 3
CONV_HALO = 32
GATE_PAD = LANES
IN_COLS_P = 2 * GLA_KEY + 2 * GLA_VAL + 2 * CONV_CH + GATE_PAD
VMEM_LIMIT = 56 * 1024 * 1024
SUBLANES = 8
REC_ROWS = D_MODEL // LANES

_OQ, _OK, _OV, _OG = 0, GLA_KEY, 2 * GLA_KEY, 2 * GLA_KEY + GLA_VAL
_OCA = _OG + GLA_VAL
_OCB = _OCA + CONV_CH
_OGR = _OCB + CONV_CH

_LEVELS = (32, 16, 8, 4, 2, 1)


def _rms(x, w):
    return x * lax.rsqrt(jnp.mean(x * x, axis=-1, keepdims=True) + EPS) * w


def _bf(x):
    return x.astype(jnp.bfloat16)


def _dot(a, b):
    return jnp.dot(a, b, preferred_element_type=jnp.float32)


def _boundary(b, hs, rows):
    n, c = b.shape
    blk = 2 * hs
    if blk >= 8:
        b3 = b.reshape(n // blk, blk, c)
        return jnp.broadcast_to(b3[:, hs - 1:hs, :], (n // blk, blk, c)).reshape(n, c)
    y = pltpu.roll(b, n - (hs - 1), 0) if hs > 1 else b
    s = 1
    while s < blk:
        y = jnp.where((rows & s) != 0, pltpu.roll(y, s, 0), y)
        s *= 2
    return y


def _mixer_kernel(x_ref, ln_ref, win_ref, wgu_ref, bg_ref, gnw_ref, cw_ref, cb_ref, cnw_ref, cnb_ref,
                  wout_ref, o_ref, s_ref, ubuf_ref, ush_ref, q_ref, y_ref):
    tt = x_ref.shape[1]
    nch = tt // CHUNK
    t = pl.program_id(1)

    @pl.when(t == 0)
    def _():
        s_ref[...] = jnp.zeros_like(s_ref)
        ubuf_ref[0:CONV_HALO, :] = jnp.zeros((CONV_HALO, CONV_CH), jnp.float32)
        ubuf_ref[CONV_HALO + tt:, :] = jnp.zeros((SUBLANES, CONV_CH), jnp.float32)

    x = x_ref[0]
    h = _bf(_rms(x, ln_ref[...]))
    z = _dot(h, win_ref[...])

    q = z[:, _OQ:_OQ + GLA_KEY] * (GLA_DK ** -0.5)
    k = z[:, _OK:_OK + GLA_KEY]
    v = z[:, _OV:_OV + GLA_VAL]
    g = z[:, _OG:_OG + GLA_VAL]
    ca = z[:, _OCA:_OCA + CONV_CH]
    cb = z[:, _OCB:_OCB + CONV_CH]
    gr = z[:, _OGR:_OGR + GATE_PAD]

    ubuf_ref[CONV_HALO:CONV_HALO + tt, :] = ca * jax.nn.sigmoid(cb)
    off0 = CONV_HALO - (CONV_WIDTH - 1)
    half = SUBLANES // 2
    ufull = ubuf_ref[...]
    n_u = ufull.shape[0]
    for r in range(1, half):
        ush_ref[r - 1] = pltpu.roll(ufull, n_u - r, 0)[0:n_u - SUBLANES, :]

    def taps(rows0, n_rows, c0, acc, upper):
        for j in range(CONV_WIDTH):
            r = (off0 + j) % SUBLANES
            if (r >= half) != upper:
                continue
            a0 = rows0 + off0 + j - r
            src = ubuf_ref if r % half == 0 else ush_ref.at[r % half - 1]
            acc = acc + cw_ref[j:j + 1, c0:c0 + LANES] * src[a0:a0 + n_rows, c0:c0 + LANES]
        return acc

    for c0 in range(0, CONV_CH, LANES):
        for r0 in list(range(0, tt, CHUNK)) + [tt]:
            n_rows = CHUNK if r0 < tt else SUBLANES
            q_ref[r0:r0 + n_rows, c0:c0 + LANES] = taps(r0, n_rows, c0, jnp.zeros((n_rows, LANES), jnp.float32),
                                                        True)
    for r0 in range(0, tt, CHUNK):
        for c0 in range(0, CONV_CH, LANES):
            acc = jnp.broadcast_to(cb_ref[:, c0:c0 + LANES], (CHUNK, LANES))
            acc = taps(r0, CHUNK, c0, acc, False)
            y_ref[r0:r0 + CHUNK, c0:c0 + LANES] = acc + q_ref[r0 + half:r0 + half + CHUNK, c0:c0 + LANES]
    ubuf_ref[0:CONV_HALO, :] = ubuf_ref[tt:tt + CONV_HALO, :]
    yc = y_ref[...]
    mu = jnp.mean(yc, axis=-1, keepdims=True)
    yd = yc - mu
    var = jnp.mean(yd * yd, axis=-1, keepdims=True)
    u = yd * lax.rsqrt(var + EPS) * cnw_ref[...] + cnb_ref[...]
    u = u * jax.nn.sigmoid(u)

    logit = _dot(_bf(gr), wgu_ref[...]) + bg_ref[...]
    la = jax.nn.log_sigmoid(logit) * (1.0 / GATE_NORMALIZER)
    rows = lax.broadcasted_iota(jnp.int32, (tt, GLA_KEY), 0)
    la_hi = _bf(la)
    la_r = la - la_hi.astype(jnp.float32)
    la_mid = _bf(la_r)
    la_lo = _bf(la_r - la_mid.astype(jnp.float32))
    tri = (lax.broadcasted_iota(jnp.int32, (CHUNK, CHUNK), 1)
           <= lax.broadcasted_iota(jnp.int32, (CHUNK, CHUNK), 0)).astype(jnp.bfloat16)
    b = jnp.concatenate(
        [_dot(tri, la_hi[c * CHUNK:(c + 1) * CHUNK]) + (_dot(tri, la_mid[c * CHUNK:(c + 1) * CHUNK])
                                                        + _dot(tri, la_lo[c * CHUNK:(c + 1) * CHUNK]))
         for c in range(nch)], axis=0)
    b3 = b.reshape(nch, CHUNK, GLA_KEY)
    blast3 = b3[:, CHUNK - 1:CHUNK, :]
    blast = jnp.broadcast_to(blast3, (nch, CHUNK, GLA_KEY)).reshape(tt, GLA_KEY)
    qe = _bf(q * jnp.exp(b))
    kl = _bf(k * jnp.exp(blast - b))
    vb = _bf(v)
    dl = jnp.exp(blast3.reshape(nch, GLA_KEY))
    dl_t = jnp.transpose(jnp.concatenate([dl, jnp.zeros((LANES - nch, GLA_KEY), jnp.float32)], axis=0))

    qh, kh = [_bf(q)], [_bf(k)]
    for hs in _LEVELS:
        e = jnp.exp(-jnp.abs(b - _boundary(b, hs, rows)))
        qh.append(_bf(q * e))
        kh.append(_bf(k * e))

    ii = lax.broadcasted_iota(jnp.int32, (CHUNK, GLA_HEADS * CHUNK), 0)
    jj = lax.broadcasted_iota(jnp.int32, (CHUNK, GLA_HEADS * CHUNK), 1) & (CHUNK - 1)
    masks = [ii == jj]
    for hs in _LEVELS:
        blk = 2 * hs
        masks.append(((ii // blk) == (jj // blk)) & ((ii & (blk - 1)) >= hs) & ((jj & (blk - 1)) < hs))
    rk = lax.broadcasted_iota(jnp.int32, (GLA_HEADS * CHUNK, GLA_KEY), 0) // CHUNK
    ck = lax.broadcasted_iota(jnp.int32, (GLA_HEADS * CHUNK, GLA_KEY), 1) // GLA_DK
    bd_k = (rk == ck).astype(jnp.bfloat16)
    rv = lax.broadcasted_iota(jnp.int32, (GLA_HEADS * CHUNK, GLA_VAL), 0) // CHUNK
    cv = lax.broadcasted_iota(jnp.int32, (GLA_HEADS * CHUNK, GLA_VAL), 1) // GLA_DV
    bd_v = rv == cv
    bd_vb = bd_v.astype(jnp.bfloat16)

    st = s_ref[...]
    o_parts = []
    for c in range(nch):
        sl = slice(c * CHUNK, (c + 1) * CHUNK)
        att = jnp.zeros((CHUNK, GLA_HEADS * CHUNK), jnp.float32)
        for lvl in range(len(masks)):
            kbd = jnp.concatenate([kh[lvl][sl]] * GLA_HEADS, axis=0) * bd_k
            sc = lax.dot_general(qh[lvl][sl], kbd, (((1,), (1,)), ((), ())),
                                 preferred_element_type=jnp.float32)
            att = jnp.where(masks[lvl], sc, att)
        vbd = jnp.concatenate([vb[sl]] * GLA_HEADS, axis=0) * bd_vb
        o_parts.append(_dot(_bf(att), vbd) + _dot(qe[sl], _bf(st)))
        upd = lax.dot_general(kl[sl], vb[sl], (((0,), (0,)), ((), ())), preferred_element_type=jnp.float32)
        st = st * dl_t[:, c:c + 1] + jnp.where(bd_v, upd, 0.0)
    s_ref[...] = st
    o = jnp.concatenate(o_parts, axis=0)

    gnw = gnw_ref[...]
    heads = []
    for hd in range(GLA_HEADS):
        oh = o[:, hd * GLA_DV:(hd + 1) * GLA_DV]
        heads.append(_rms(oh, gnw))
    o = jnp.concatenate(heads, axis=-1) * (g * jax.nn.sigmoid(g))

    mix = _bf(jnp.concatenate([o, u], axis=-1))
    o_ref[0] = x + _dot(mix, wout_ref[...])


def _mixer_call(x, layer, ln_w, w_in_p, wgu_p, b_gate, gnw, conv_w, conv_b, cn_w, cn_b, w_out_b):
    bsz, seq, d = x.shape
    tt = min(TIME_TILE, seq)
    assert seq % tt == 0 and tt % CHUNK == 0
    const = lambda *shape: pl.BlockSpec((None,) + shape, lambda b, t: (layer,) + (0,) * len(shape))
    return pl.pallas_call(
        _mixer_kernel,
        out_shape=jax.ShapeDtypeStruct(x.shape, x.dtype),
        grid=(bsz, seq // tt),
        in_specs=[
            pl.BlockSpec((1, tt, d), lambda b, t: (b, t, 0)),
            const(1, d), const(d, IN_COLS_P), const(GATE_PAD, GLA_KEY), const(1, GLA_KEY), const(1, GLA_DV),
            const(CONV_HALO, CONV_CH), const(1, CONV_CH), const(1, CONV_CH), const(1, CONV_CH),
            const(GLA_VAL + CONV_CH, d),
        ],
        out_specs=pl.BlockSpec((1, tt, d), lambda b, t: (b, t, 0)),
        scratch_shapes=[
            pltpu.VMEM((GLA_HEADS * GLA_DK, GLA_VAL), jnp.float32),
            pltpu.VMEM((CONV_HALO + tt + SUBLANES, CONV_CH), jnp.float32),
            pltpu.VMEM((SUBLANES // 2 - 1, CONV_HALO + tt, CONV_CH), jnp.float32),
            pltpu.VMEM((tt + SUBLANES, CONV_CH), jnp.float32),
            pltpu.VMEM((tt, CONV_CH), jnp.float32),
        ],
        compiler_params=pltpu.CompilerParams(dimension_semantics=("arbitrary", "arbitrary"),
                                             vmem_limit_bytes=VMEM_LIMIT),
        name="mixer",
    )(x, ln_w, w_in_p, wgu_p, b_gate, gnw, conv_w, conv_b, cn_w, cn_b, w_out_b)


def _swiglu_chunk(h, wg_ref, wu_ref, wd_ref, f0):
    gt = _dot(h, wg_ref[:, f0:f0 + FF_CHUNK])
    up = _dot(h, wu_ref[:, f0:f0 + FF_CHUNK])
    a = _bf(gt * jax.nn.sigmoid(gt) * up)
    return _dot(a, wd_ref[f0:f0 + FF_CHUNK, :])


def _swiglu(h, wg_ref, wu_ref, wd_ref):
    y = None
    for f0 in range(0, D_FF, FF_CHUNK):
        part = _swiglu_chunk(h, wg_ref, wu_ref, wd_ref, f0)
        y = part if y is None else y + part
    return y


def _ffn_dense_kernel(x_ref, ln_ref, fin_ref, wg_ref, wu_ref, wd_ref, o_ref, *, final_norm):
    x = x_ref[...]
    out = x + _swiglu(_bf(_rms(x, ln_ref[...])), wg_ref.at[0], wu_ref.at[0], wd_ref.at[0])
    if final_norm:
        out = _rms(out, fin_ref[...])
    o_ref[...] = out


def _ffn_dense_call(x2, ln_w, fin_w, wg, wu, wd, *, final_norm):
    n, d = x2.shape
    tm = min(ROW_TILE, n)
    assert n % tm == 0
    return pl.pallas_call(
        functools.partial(_ffn_dense_kernel, final_norm=final_norm),
        out_shape=jax.ShapeDtypeStruct(x2.shape, x2.dtype),
        grid=(n // tm,),
        in_specs=[
            pl.BlockSpec((tm, d), lambda i: (i, 0)),
            pl.BlockSpec((1, d), lambda i: (0, 0)),
            pl.BlockSpec((1, d), lambda i: (0, 0)),
            pl.BlockSpec((1, d, D_FF), lambda i: (0, 0, 0)),
            pl.BlockSpec((1, d, D_FF), lambda i: (0, 0, 0)),
            pl.BlockSpec((1, D_FF, d), lambda i: (0, 0, 0)),
        ],
        out_specs=pl.BlockSpec((tm, d), lambda i: (i, 0)),
        compiler_params=pltpu.CompilerParams(dimension_semantics=("arbitrary",), vmem_limit_bytes=VMEM_LIMIT),
        name="ffn_dense",
    )(x2, ln_w, fin_w, wg, wu, wd)


def _store_records(rec_ref, vals):
    m = vals.shape[0]
    for s in range(REC_ROWS):
        rec_ref[pl.ds(s, m, stride=REC_ROWS), :] = vals[:, s * LANES:(s + 1) * LANES]


def _load_records(rec_ref, m):
    return [rec_ref[pl.ds(s, m, stride=REC_ROWS), :] for s in range(REC_ROWS)]


def _copy_records(src_ref, src_off, dst_ref, dst_off, n, sem, max_rows, wait=False):
    bit = max_rows.bit_length() - 1
    while bit >= 0:
        size = (1 << bit) * REC_ROWS
        done = lax.shift_left(lax.shift_right_logical(n, bit + 1), bit + 1)
        src0 = 0 if src_off is None else pl.multiple_of((src_off + done) * REC_ROWS, REC_ROWS)
        dst0 = pl.multiple_of((dst_off + done) * REC_ROWS, REC_ROWS)

        @pl.when((lax.shift_right_logical(n, bit) & 1) == 1)
        def _(size=size, src0=src0, dst0=dst0):
            cp = pltpu.make_async_copy(src_ref.at[pl.ds(src0, size)], dst_ref.at[pl.ds(dst0, size)], sem)
            cp.wait() if wait else cp.start()
        bit -= 1


def _one_hot_rows(pos0, pos1, n_rows):
    r = lax.broadcasted_iota(jnp.int32, (n_rows, pos0.shape[1]), 0)
    return jnp.where((r == pos0) | (r == pos1), 1.0, 0.0).astype(jnp.bfloat16)


def _route_kernel(x_ref, ln_ref, wrh_ref, wrl_ref, pos_ref, gate_ref, cnt_ref):
    ts = x_ref.shape[0]
    t = pl.program_id(0)
    h = _rms(x_ref[...], ln_ref[...])
    h_hi = _bf(h)
    h_lo = _bf(h - h_hi.astype(jnp.float32))
    logits = _dot(h_hi, wrh_ref[...]) + (_dot(h_hi, wrl_ref[...]) + _dot(h_lo, wrh_ref[...]))
    lt = jnp.transpose(logits)[0:N_EXPERTS, :]
    row = lax.broadcasted_iota(jnp.int32, lt.shape, 0)
    neg = jnp.float32(-jnp.inf)
    m1 = jnp.max(lt, axis=0, keepdims=True)
    i1 = jnp.min(jnp.where(lt == m1, row, N_EXPERTS), axis=0, keepdims=True)
    rest = jnp.where(row == i1, neg, lt)
    m2 = jnp.max(rest, axis=0, keepdims=True)
    i2 = jnp.min(jnp.where(rest == m2, row, N_EXPERTS), axis=0, keepdims=True)
    e2 = jnp.exp(m2 - m1)
    den = 1.0 + e2
    g1 = 1.0 / den
    g2 = e2 / den
    sel1 = row == i1
    sel2 = row == i2
    oh = jnp.where(sel1 | sel2, 1.0, 0.0)

    sp = lax.broadcasted_iota(jnp.int32, (ts, ts), 0)
    sc = lax.broadcasted_iota(jnp.int32, (ts, ts), 1)
    upper = jnp.where(sp < sc, 1.0, 0.0).astype(jnp.bfloat16)
    rank = _dot(_bf(oh), upper)
    rk1 = jnp.sum(jnp.where(sel1, rank, 0.0), axis=0, keepdims=True)
    rk2 = jnp.sum(jnp.where(sel2, rank, 0.0), axis=0, keepdims=True)

    cnts, offs = [], []
    off = jnp.int32(0)
    for e in range(N_EXPERTS):
        c = jnp.sum(oh[e:e + 1, :]).astype(jnp.int32)
        cnts.append(c)
        offs.append(off)
        off = off + c
    off1 = jnp.zeros_like(rk1)
    off2 = jnp.zeros_like(rk2)
    for e in range(N_EXPERTS):
        fe = offs[e].astype(jnp.float32)
        off1 = jnp.where(i1 == e, fe, off1)
        off2 = jnp.where(i2 == e, fe, off2)
    pos1 = (off1 + rk1).astype(jnp.int32)
    pos2 = (off2 + rk2).astype(jnp.int32)
    pos_ref[0] = jnp.concatenate([pos1, pos2, jnp.zeros((SUBLANES - TOP_K, ts), jnp.int32)], axis=0)
    gate_ref[0] = jnp.concatenate([g1, g2, jnp.zeros((SUBLANES - TOP_K, ts), jnp.float32)], axis=0)
    for e in range(N_EXPERTS):
        cnt_ref[t * N_EXPERTS + e] = cnts[e]


def _route_call(x2, ln_w, wr_hi, wr_lo):
    n, d = x2.shape
    ts = min(ROW_TILE, n)
    nt = n // ts
    return pl.pallas_call(
        _route_kernel,
        out_shape=(
            jax.ShapeDtypeStruct((nt, SUBLANES, ts), jnp.int32),
            jax.ShapeDtypeStruct((nt, SUBLANES, ts), jnp.float32),
            jax.ShapeDtypeStruct((nt * N_EXPERTS,), jnp.int32),
        ),
        grid=(nt,),
        in_specs=[
            pl.BlockSpec((ts, d), lambda i: (i, 0)),
            pl.BlockSpec((1, d), lambda i: (0, 0)),
            pl.BlockSpec((d, LANES), lambda i: (0, 0)),
            pl.BlockSpec((d, LANES), lambda i: (0, 0)),
        ],
        out_specs=(
            pl.BlockSpec((1, SUBLANES, ts), lambda i: (i, 0, 0)),
            pl.BlockSpec((1, SUBLANES, ts), lambda i: (i, 0, 0)),
            pl.BlockSpec(memory_space=pltpu.SMEM),
        ),
        compiler_params=pltpu.CompilerParams(dimension_semantics=("arbitrary",), vmem_limit_bytes=VMEM_LIMIT),
        name="route",
    )(x2, ln_w, wr_hi, wr_lo)


def _dispatch_kernel(seg_ref, cnt_ref, zstart_ref, zlen_ref, nval_ref, x_ref, ln_ref, pos_ref, xs_hbm,
                     stage_ref, zero_ref, sem, zsem, *, max_tiles):
    ts = x_ref.shape[0]
    n_sorted = TOP_K * ts
    t = pl.program_id(0)
    nt = pl.num_programs(0)
    h = _bf(_rms(x_ref[...], ln_ref[...]))
    pos = pos_ref[0]
    perm = _one_hot_rows(pos[0:1, :], pos[1:2, :], n_sorted)
    sorted_h = _dot(perm, h)

    @pl.when(t > 0)
    def _():
        pltpu.make_async_copy(stage_ref, xs_hbm.at[pl.ds(0, n_sorted * REC_ROWS)], sem).wait()

    _store_records(stage_ref, sorted_h)

    off = jnp.int32(0)
    for e in range(N_EXPERTS):
        c = cnt_ref[t * N_EXPERTS + e]
        _copy_records(stage_ref, off, xs_hbm, seg_ref[t * N_EXPERTS + e], c, sem, ts)
        off = off + c

    @pl.when(t == nt - 1)
    def _():
        pltpu.make_async_copy(stage_ref, xs_hbm.at[pl.ds(0, n_sorted * REC_ROWS)], sem).wait()
        tm = zero_ref.shape[0] // REC_ROWS
        zero_ref[...] = jnp.zeros_like(zero_ref)

        def tail_fill(k):
            return pltpu.make_async_copy(
                zero_ref, xs_hbm.at[pl.ds(pl.multiple_of((nval_ref[0] + k) * tm * REC_ROWS, REC_ROWS),
                                          tm * REC_ROWS)], zsem)

        for wait in (False, True):
            for e in range(N_EXPERTS):
                _copy_records(zero_ref, None, xs_hbm, zstart_ref[e], zlen_ref[e], zsem, tm, wait=wait)
            for k in range(N_EXPERTS):
                @pl.when(nval_ref[0] + k < max_tiles)
                def _(k=k, wait=wait):
                    tail_fill(k).wait() if wait else tail_fill(k).start()


def _dispatch_call(seg, cnt, zstart, zlen, n_valid, x2, ln_w, pos, max_tiles):
    n, d = x2.shape
    ts = min(ROW_TILE, n)
    nt = n // ts
    return pl.pallas_call(
        functools.partial(_dispatch_kernel, max_tiles=max_tiles),
        out_shape=jax.ShapeDtypeStruct((max_tiles * ROW_TILE * REC_ROWS, LANES), jnp.float32),
        grid_spec=pltpu.PrefetchScalarGridSpec(
            num_scalar_prefetch=5,
            grid=(nt,),
            in_specs=[
                pl.BlockSpec((ts, d), lambda i, *_: (i, 0)),
                pl.BlockSpec((1, d), lambda i, *_: (0, 0)),
                pl.BlockSpec((1, SUBLANES, ts), lambda i, *_: (i, 0, 0)),
            ],
            out_specs=pl.BlockSpec(memory_space=pl.ANY),
            scratch_shapes=[
                pltpu.VMEM((TOP_K * ts * REC_ROWS, LANES), jnp.float32),
                pltpu.VMEM((ROW_TILE * REC_ROWS, LANES), jnp.float32),
                pltpu.SemaphoreType.DMA(()),
                pltpu.SemaphoreType.DMA(()),
            ],
        ),
        compiler_params=pltpu.CompilerParams(dimension_semantics=("arbitrary",), vmem_limit_bytes=VMEM_LIMIT),
        name="dispatch",
    )(seg, cnt, zstart, zlen, n_valid, x2, ln_w, pos)


def _ffn_group_kernel(texp_ref, tnext_ref, tswap_ref, nval_ref, xs_ref, wg_hbm, wu_hbm, wd_hbm, ys_ref,
                      wg_ref, wu_ref, wd_ref, sg_ref, su_ref, sd_ref, sems):
    j = pl.program_id(0)
    n_chunks = D_FF // FF_CHUNK
    gw = SWAP_GROUP * FF_CHUNK
    groups = [(c0, min(c0 + gw, D_FF)) for c0 in range(0, D_FF, gw)]

    def group_copies(e, g):
        c0, c1 = groups[g]
        w, slot = c1 - c0, g % 2
        return (pltpu.make_async_copy(wg_hbm.at[e, :, pl.ds(c0, w)], sg_ref.at[slot, :, pl.ds(0, w)],
                                      sems.at[slot]),
                pltpu.make_async_copy(wu_hbm.at[e, :, pl.ds(c0, w)], su_ref.at[slot, :, pl.ds(0, w)],
                                      sems.at[slot]),
                pltpu.make_async_copy(wd_hbm.at[e, pl.ds(c0, w), :], sd_ref.at[slot, pl.ds(0, w), :],
                                      sems.at[slot]))

    def start(e, g):
        for cp in group_copies(e, g):
            cp.start()

    def install(e, g):
        for cp in group_copies(e, g):
            cp.wait()
        c0, c1 = groups[g]
        w, slot = c1 - c0, g % 2
        wg_ref[:, c0:c1] = _bf(sg_ref[slot, :, 0:w])
        wu_ref[:, c0:c1] = _bf(su_ref[slot, :, 0:w])
        wd_ref[c0:c1, :] = _bf(sd_ref[slot, 0:w, :])

    def load_h():
        tm = xs_ref.shape[0] // REC_ROWS
        return jnp.concatenate([_bf(w) for w in _load_records(xs_ref, tm)], axis=1)

    @pl.when(j == 0)
    def _():
        e = texp_ref[0]
        start(e, 0)
        for g in range(len(groups)):
            if g + 1 < len(groups):
                start(e, g + 1)
            install(e, g)

    valid = j < nval_ref[0]
    swap = tswap_ref[j] == 1
    not_ = jnp.logical_not

    @pl.when(not_(valid))
    def _():
        ys_ref[...] = jnp.zeros_like(ys_ref)

    @pl.when(valid & not_(swap))
    def _():
        _store_records(ys_ref, _swiglu(load_h(), wg_ref, wu_ref, wd_ref))

    @pl.when(valid & swap)
    def _():
        e = tnext_ref[j]
        start(e, 0)
        start(e, 1)
        h = load_h()
        y = None
        for f in range(n_chunks):
            part = _swiglu_chunk(h, wg_ref, wu_ref, wd_ref, f * FF_CHUNK)
            y = part if y is None else y + part
            g = f // SWAP_GROUP
            if (f + 1) * FF_CHUNK == groups[g][1]:
                install(e, g)
                if g + 2 < len(groups):
                    start(e, g + 2)
        _store_records(ys_ref, y)


def _ffn_group_call(tile_expert, tile_next, tile_swap, n_valid, xs, wg, wu, wd):
    rows = xs.shape[0]
    tm = ROW_TILE
    n_tiles = tile_expert.shape[0]
    d = wg.shape[1]
    return pl.pallas_call(
        _ffn_group_kernel,
        out_shape=jax.ShapeDtypeStruct((rows, LANES), jnp.float32),
        grid_spec=pltpu.PrefetchScalarGridSpec(
            num_scalar_prefetch=4,
            grid=(n_tiles,),
            in_specs=[
                pl.BlockSpec((tm * REC_ROWS, LANES), lambda j, *_: (j, 0)),
                pl.BlockSpec(memory_space=pl.ANY),
                pl.BlockSpec(memory_space=pl.ANY),
                pl.BlockSpec(memory_space=pl.ANY),
            ],
            out_specs=pl.BlockSpec((tm * REC_ROWS, LANES), lambda j, *_: (j, 0)),
            scratch_shapes=[
                pltpu.VMEM((d, D_FF), jnp.bfloat16),
                pltpu.VMEM((d, D_FF), jnp.bfloat16),
                pltpu.VMEM((D_FF, d), jnp.bfloat16),
                pltpu.VMEM((2, d, SWAP_GROUP * FF_CHUNK), jnp.float32),
                pltpu.VMEM((2, d, SWAP_GROUP * FF_CHUNK), jnp.float32),
                pltpu.VMEM((2, SWAP_GROUP * FF_CHUNK, d), jnp.float32),
                pltpu.SemaphoreType.DMA((2,)),
            ],
        ),
        compiler_params=pltpu.CompilerParams(dimension_semantics=("arbitrary",), vmem_limit_bytes=VMEM_LIMIT),
        name="ffn_group",
    )(tile_expert, tile_next, tile_swap, n_valid, xs, wg, wu, wd)


def _combine_kernel(seg_ref, cnt_ref, x_ref, fin_ref, pos_ref, gate_ref, ys_hbm, o_ref, ybuf_ref, sems,
                    *, final_norm):
    ts = x_ref.shape[0]
    n_sorted = TOP_K * ts
    t = pl.program_id(0)
    nt = pl.num_programs(0)

    def fetch(tile, slot):
        off = jnp.int32(0)
        for e in range(N_EXPERTS):
            c = cnt_ref[tile * N_EXPERTS + e]
            _copy_records(ys_hbm, seg_ref[tile * N_EXPERTS + e], ybuf_ref.at[slot], off, c, sems.at[slot], ts)
            off = off + c

    @pl.when(t == 0)
    def _():
        fetch(0, 0)

    @pl.when(t + 1 < nt)
    def _():
        fetch(t + 1, (t + 1) % 2)

    slot = t % 2
    pltpu.make_async_copy(ys_hbm.at[pl.ds(0, n_sorted * REC_ROWS)], ybuf_ref.at[slot], sems.at[slot]).wait()
    ys = jnp.concatenate([_bf(w) for w in _load_records(ybuf_ref.at[slot], n_sorted)], axis=1)
    pos = pos_ref[0]
    gates = gate_ref[0]
    r = lax.broadcasted_iota(jnp.int32, (n_sorted, ts), 0)
    sel = _bf(jnp.where(r == pos[0:1, :], gates[0:1, :], 0.0) + jnp.where(r == pos[1:2, :], gates[1:2, :], 0.0))
    y = lax.dot_general(sel, ys, (((0,), (0,)), ((), ())), preferred_element_type=jnp.float32)
    out = x_ref[...] + y
    if final_norm:
        out = _rms(out, fin_ref[...])
    o_ref[...] = out


def _combine_call(seg, cnt, x2, fin_w, pos, gates, ys, *, final_norm):
    n, d = x2.shape
    ts = min(ROW_TILE, n)
    nt = n // ts
    return pl.pallas_call(
        functools.partial(_combine_kernel, final_norm=final_norm),
        out_shape=jax.ShapeDtypeStruct(x2.shape, x2.dtype),
        grid_spec=pltpu.PrefetchScalarGridSpec(
            num_scalar_prefetch=2,
            grid=(nt,),
            in_specs=[
                pl.BlockSpec((ts, d), lambda i, sg, ct: (i, 0)),
                pl.BlockSpec((1, d), lambda i, sg, ct: (0, 0)),
                pl.BlockSpec((1, SUBLANES, ts), lambda i, sg, ct: (i, 0, 0)),
                pl.BlockSpec((1, SUBLANES, ts), lambda i, sg, ct: (i, 0, 0)),
                pl.BlockSpec(memory_space=pl.ANY),
            ],
            out_specs=pl.BlockSpec((ts, d), lambda i, sg, ct: (i, 0)),
            scratch_shapes=[
                pltpu.VMEM((2, TOP_K * ts * REC_ROWS, LANES), jnp.float32),
                pltpu.SemaphoreType.DMA((2,)),
            ],
        ),
        compiler_params=pltpu.CompilerParams(dimension_semantics=("arbitrary",), vmem_limit_bytes=VMEM_LIMIT),
        name="combine",
    )(seg, cnt, x2, fin_w, pos, gates, ys)


def _moe(x2, ln_w, fin_w, w_router, wg, wu, wd, *, final_norm):
    n, d = x2.shape
    tm = ROW_TILE
    wr_p = jnp.concatenate([w_router, jnp.zeros((d, LANES - N_EXPERTS), w_router.dtype)], axis=1)
    wr_hi = _bf(wr_p)
    wr_lo = _bf(wr_p - wr_hi.astype(jnp.float32))
    pos, gates, cnt = _route_call(x2, ln_w, wr_hi, wr_lo)

    i32 = jnp.int32
    nt = cnt.shape[0] // N_EXPERTS
    cnt2 = cnt.reshape(nt, N_EXPERTS)
    totals = jnp.sum(cnt2, axis=0)
    tiles_e = (totals + tm - 1) // tm
    ends = jnp.cumsum(tiles_e)
    base = (ends - tiles_e) * tm
    seg = (base[None, :] + jnp.cumsum(cnt2, axis=0) - cnt2).reshape(-1).astype(i32)
    zstart = (base + totals).astype(i32)
    zlen = (tiles_e * tm - totals).astype(i32)
    n_valid = ends[-1].reshape(1).astype(i32)
    max_tiles = (TOP_K * n) // tm + N_EXPERTS
    j = jnp.minimum(jnp.arange(max_tiles, dtype=i32), n_valid - 1)
    tile_expert = jnp.sum((j[:, None] >= ends[None, :]).astype(i32), axis=1).astype(i32)
    tile_next = jnp.concatenate([tile_expert[1:], tile_expert[-1:]])
    tile_swap = (tile_next != tile_expert).astype(i32)

    xs = _dispatch_call(seg, cnt, zstart, zlen, n_valid, x2, ln_w, pos, max_tiles)
    ys = _ffn_group_call(tile_expert, tile_next, tile_swap, n_valid, xs, wg, wu, wd)
    return _combine_call(seg, cnt, x2, fin_w, pos, gates, ys, final_norm=final_norm)


def _prep_w_in(w):
    depth, d, _ = w.shape
    o_gr = 2 * GLA_KEY + 2 * GLA_VAL
    parts = [w[:, :, :o_gr], w[:, :, o_gr + GATE_RANK:], w[:, :, o_gr:o_gr + GATE_RANK],
             jnp.zeros((depth, d, GATE_PAD - GATE_RANK), w.dtype)]
    return _bf(jnp.concatenate(parts, axis=2))


def kernel(x, ln1_w, w_in, w_gate_up, b_gate, gla_norm_w, conv_w, conv_b, cn_w, cn_b, w_out, ln2_w, wd_gate,
           wd_up, wd_down, w_router, we_gate, we_up, we_down, final_norm_w):
    bsz, seq, d = x.shape
    depth = ln1_w.shape[0]
    n = bsz * seq
    row = lambda a: a.reshape(1, -1)
    rows = lambda a: a[:, None, :]
    w_in_p = _prep_w_in(w_in)
    wgu_p = _bf(jnp.concatenate(
        [w_gate_up, jnp.zeros((depth, GATE_PAD - GATE_RANK, GLA_KEY), w_gate_up.dtype)], axis=1))
    cw_p = jnp.concatenate([conv_w, jnp.zeros((depth, CONV_HALO - CONV_WIDTH, CONV_CH), conv_w.dtype)], axis=1)
    w_out_b = _bf(w_out)
    for l in range(depth):
        x = _mixer_call(x, l, rows(ln1_w), w_in_p, wgu_p, rows(b_gate), rows(gla_norm_w), cw_p, rows(conv_b),
                        rows(cn_w), rows(cn_b), w_out_b)
        x2 = x.reshape(n, d)
        last = l == depth - 1
        i = l // 2
        if l % 2 == 0:
            x2 = _ffn_dense_call(x2, row(ln2_w[l]), row(final_norm_w), _bf(wd_gate[i:i + 1]),
                                 _bf(wd_up[i:i + 1]), _bf(wd_down[i:i + 1]), final_norm=last)
        else:
            x2 = _moe(x2, row(ln2_w[l]), row(final_norm_w), w_router[i], we_gate[i], we_up[i], we_down[i],
                      final_norm=last)
        x = x2.reshape(bsz, seq, d)
    return x
```

```python
import functools

import jax
import jax.numpy as jnp
from jax import lax
from jax.experimental import pallas as pl
from jax.experimental.pallas import tpu as pltpu

D_MODEL = 1024
GLA_HEADS = 4
GLA_DK = 64
GLA_DV = 128
GLA_KEY = GLA_HEADS * GLA_DK
GLA_VAL = GLA_HEADS * GLA_DV
GATE_RANK = 16
GATE_NORMALIZER = 16.0
CONV_CH = 512
CONV_WIDTH = 31
D_FF = 2816
N_EXPERTS = 8
TOP_K = 2
EPS = 1e-6

LANES = 128
CHUNK = 64
TIME_TILE = 512
ROW_TILE = 512
FF_CHUNK = 256
SWAP_GROUP = 3
CONV_HALO = 32
GATE_PAD = LANES
IN_COLS_P = 2 * GLA_KEY + 2 * GLA_VAL + 2 * CONV_CH + GATE_PAD
VMEM_LIMIT = 56 * 1024 * 1024
SUBLANES = 8
REC_ROWS = D_MODEL // LANES

_OQ, _OK, _OV, _OG = 0, GLA_KEY, 2 * GLA_KEY, 2 * GLA_KEY + GLA_VAL
_OCA = _OG + GLA_VAL
_OCB = _OCA + CONV_CH
_OGR = _OCB + CONV_CH

_LEVELS = (32, 16, 8, 4, 2, 1)


def _rms(x, w):
    return x * lax.rsqrt(jnp.mean(x * x, axis=-1, keepdims=True) + EPS) * w


def _bf(x):
    return x.astype(jnp.bfloat16)


def _dot(a, b):
    return jnp.dot(a, b, preferred_element_type=jnp.float32)


def _boundary(b, hs, rows):
    n, c = b.shape
    blk = 2 * hs
    if blk >= 8:
        b3 = b.reshape(n // blk, blk, c)
        return jnp.broadcast_to(b3[:, hs - 1:hs, :], (n // blk, blk, c)).reshape(n, c)
    y = pltpu.roll(b, n - (hs - 1), 0) if hs > 1 else b
    s = 1
    while s < blk:
        y = jnp.where((rows & s) != 0, pltpu.roll(y, s, 0), y)
        s *= 2
    return y


def _mixer_kernel(x_ref, ln_ref, win_ref, wgu_ref, bg_ref, gnw_ref, cw_ref, cb_ref, cnw_ref, cnb_ref,
                  wout_ref, o_ref, s_ref, ubuf_ref, ush_ref, q_ref, y_ref):
    tt = x_ref.shape[1]
    nch = tt // CHUNK
    t = pl.program_id(1)

    @pl.when(t == 0)
    def _():
        s_ref[...] = jnp.zeros_like(s_ref)
        ubuf_ref[0:CONV_HALO, :] = jnp.zeros((CONV_HALO, CONV_CH), jnp.float32)
        ubuf_ref[CONV_HALO + tt:, :] = jnp.zeros((SUBLANES, CONV_CH), jnp.float32)

    x = x_ref[0]
    h = _bf(_rms(x, ln_ref[...]))
    z = _dot(h, win_ref[...])

    q = z[:, _OQ:_OQ + GLA_KEY] * (GLA_DK ** -0.5)
    k = z[:, _OK:_OK + GLA_KEY]
    v = z[:, _OV:_OV + GLA_VAL]
    g = z[:, _OG:_OG + GLA_VAL]
    ca = z[:, _OCA:_OCA + CONV_CH]
    cb = z[:, _OCB:_OCB + CONV_CH]
    gr = z[:, _OGR:_OGR + GATE_PAD]

    ubuf_ref[CONV_HALO:CONV_HALO + tt, :] = ca * jax.nn.sigmoid(cb)
    off0 = CONV_HALO - (CONV_WIDTH - 1)
    half = SUBLANES // 2
    ufull = ubuf_ref[...]
    n_u = ufull.shape[0]
    for r in range(1, half):
        ush_ref[r - 1] = pltpu.roll(ufull, n_u - r, 0)[0:n_u - SUBLANES, :]

    def taps(rows0, n_rows, c0, acc, upper):
        for j in range(CONV_WIDTH):
            r = (off0 + j) % SUBLANES
            if (r >= half) != upper:
                continue
            a0 = rows0 + off0 + j - r
            src = ubuf_ref if r % half == 0 else ush_ref.at[r % half - 1]
            acc = acc + cw_ref[j:j + 1, c0:c0 + LANES] * src[a0:a0 + n_rows, c0:c0 + LANES]
        return acc

    for c0 in range(0, CONV_CH, LANES):
        for r0 in list(range(0, tt, CHUNK)) + [tt]:
            n_rows = CHUNK if r0 < tt else SUBLANES
            q_ref[r0:r0 + n_rows, c0:c0 + LANES] = taps(r0, n_rows, c0, jnp.zeros((n_rows, LANES), jnp.float32),
                                                        True)
    for r0 in range(0, tt, CHUNK):
        for c0 in range(0, CONV_CH, LANES):
            acc = jnp.broadcast_to(cb_ref[:, c0:c0 + LANES], (CHUNK, LANES))
            acc = taps(r0, CHUNK, c0, acc, False)
            y_ref[r0:r0 + CHUNK, c0:c0 + LANES] = acc + q_ref[r0 + half:r0 + half + CHUNK, c0:c0 + LANES]
    ubuf_ref[0:CONV_HALO, :] = ubuf_ref[tt:tt + CONV_HALO, :]
    yc = y_ref[...]
    mu = jnp.mean(yc, axis=-1, keepdims=True)
    yd = yc - mu
    var = jnp.mean(yd * yd, axis=-1, keepdims=True)
    u = yd * lax.rsqrt(var + EPS) * cnw_ref[...] + cnb_ref[...]
    u = u * jax.nn.sigmoid(u)

    logit = _dot(_bf(gr), wgu_ref[...]) + bg_ref[...]
    la = jax.nn.log_sigmoid(logit) * (1.0 / GATE_NORMALIZER)
    rows = lax.broadcasted_iota(jnp.int32, (tt, GLA_KEY), 0)
    la_hi = _bf(la)
    la_r = la - la_hi.astype(jnp.float32)
    la_mid = _bf(la_r)
    la_lo = _bf(la_r - la_mid.astype(jnp.float32))
    tri = (lax.broadcasted_iota(jnp.int32, (CHUNK, CHUNK), 1)
           <= lax.broadcasted_iota(jnp.int32, (CHUNK, CHUNK), 0)).astype(jnp.bfloat16)
    b = jnp.concatenate(
        [_dot(tri, la_hi[c * CHUNK:(c + 1) * CHUNK]) + (_dot(tri, la_mid[c * CHUNK:(c + 1) * CHUNK])
                                                        + _dot(tri, la_lo[c * CHUNK:(c + 1) * CHUNK]))
         for c in range(nch)], axis=0)
    b3 = b.reshape(nch, CHUNK, GLA_KEY)
    blast3 = b3[:, CHUNK - 1:CHUNK, :]
    blast = jnp.broadcast_to(blast3, (nch, CHUNK, GLA_KEY)).reshape(tt, GLA_KEY)
    qe = _bf(q * jnp.exp(b))
    kl = _bf(k * jnp.exp(blast - b))
    vb = _bf(v)
    dl = jnp.exp(blast3.reshape(nch, GLA_KEY))
    dl_t = jnp.transpose(jnp.concatenate([dl, jnp.zeros((LANES - nch, GLA_KEY), jnp.float32)], axis=0))

    qh, kh = [_bf(q)], [_bf(k)]
    for hs in _LEVELS:
        e = jnp.exp(-jnp.abs(b - _boundary(b, hs, rows)))
        qh.append(_bf(q * e))
        kh.append(_bf(k * e))

    ii = lax.broadcasted_iota(jnp.int32, (CHUNK, GLA_HEADS * CHUNK), 0)
    jj = lax.broadcasted_iota(jnp.int32, (CHUNK, GLA_HEADS * CHUNK), 1) & (CHUNK - 1)
    masks = [ii == jj]
    for hs in _LEVELS:
        blk = 2 * hs
        masks.append(((ii // blk) == (jj // blk)) & ((ii & (blk - 1)) >= hs) & ((jj & (blk - 1)) < hs))
    rk = lax.broadcasted_iota(jnp.int32, (GLA_HEADS * CHUNK, GLA_KEY), 0) // CHUNK
    ck = lax.broadcasted_iota(jnp.int32, (GLA_HEADS * CHUNK, GLA_KEY), 1) // GLA_DK
    bd_k = (rk == ck).astype(jnp.bfloat16)
    rv = lax.broadcasted_iota(jnp.int32, (GLA_HEADS * CHUNK, GLA_VAL), 0) // CHUNK
    cv = lax.broadcasted_iota(jnp.int32, (GLA_HEADS * CHUNK, GLA_VAL), 1) // GLA_DV
    bd_v = rv == cv
    bd_vb = bd_v.astype(jnp.bfloat16)

    st = s_ref[...]
    o_parts = []
    for c in range(nch):
        sl = slice(c * CHUNK, (c + 1) * CHUNK)
        att = jnp.zeros((CHUNK, GLA_HEADS * CHUNK), jnp.float32)
        for lvl in range(len(masks)):
            kbd = jnp.concatenate([kh[lvl][sl]] * GLA_HEADS, axis=0) * bd_k
            sc = lax.dot_general(qh[lvl][sl], kbd, (((1,), (1,)), ((), ())),
                                 preferred_element_type=jnp.float32)
            att = jnp.where(masks[lvl], sc, att)
        vbd = jnp.concatenate([vb[sl]] * GLA_HEADS, axis=0) * bd_vb
        o_parts.append(_dot(_bf(att), vbd) + _dot(qe[sl], _bf(st)))
        upd = lax.dot_general(kl[sl], vb[sl], (((0,), (0,)), ((), ())), preferred_element_type=jnp.float32)
        st = st * dl_t[:, c:c + 1] + jnp.where(bd_v, upd, 0.0)
    s_ref[...] = st
    o = jnp.concatenate(o_parts, axis=0)

    gnw = gnw_ref[...]
    heads = []
    for hd in range(GLA_HEADS):
        oh = o[:, hd * GLA_DV:(hd + 1) * GLA_DV]
        heads.append(_rms(oh, gnw))
    o = jnp.concatenate(heads, axis=-1) * (g * jax.nn.sigmoid(g))

    mix = _bf(jnp.concatenate([o, u], axis=-1))
    o_ref[0] = x + _dot(mix, wout_ref[...])


def _mixer_call(x, layer, ln_w, w_in_p, wgu_p, b_gate, gnw, conv_w, conv_b, cn_w, cn_b, w_out_b):
    bsz, seq, d = x.shape
    tt = min(TIME_TILE, seq)
    assert seq % tt == 0 and tt % CHUNK == 0
    const = lambda *shape: pl.BlockSpec((None,) + shape, lambda b, t: (layer,) + (0,) * len(shape))
    return pl.pallas_call(
        _mixer_kernel,
        out_shape=jax.ShapeDtypeStruct(x.shape, x.dtype),
        grid=(bsz, seq // tt),
        in_specs=[
            pl.BlockSpec((1, tt, d), lambda b, t: (b, t, 0)),
            const(1, d), const(d, IN_COLS_P), const(GATE_PAD, GLA_KEY), const(1, GLA_KEY), const(1, GLA_DV),
            const(CONV_HALO, CONV_CH), const(1, CONV_CH), const(1, CONV_CH), const(1, CONV_CH),
            const(GLA_VAL + CONV_CH, d),
        ],
        out_specs=pl.BlockSpec((1, tt, d), lambda b, t: (b, t, 0)),
        scratch_shapes=[
            pltpu.VMEM((GLA_HEADS * GLA_DK, GLA_VAL), jnp.float32),
            pltpu.VMEM((CONV_HALO + tt + SUBLANES, CONV_CH), jnp.float32),
            pltpu.VMEM((SUBLANES // 2 - 1, CONV_HALO + tt, CONV_CH), jnp.float32),
            pltpu.VMEM((tt + SUBLANES, CONV_CH), jnp.float32),
            pltpu.VMEM((tt, CONV_CH), jnp.float32),
        ],
        compiler_params=pltpu.CompilerParams(dimension_semantics=("arbitrary", "arbitrary"),
                                             vmem_limit_bytes=VMEM_LIMIT),
        name="mixer",
    )(x, ln_w, w_in_p, wgu_p, b_gate, gnw, conv_w, conv_b, cn_w, cn_b, w_out_b)


def _swiglu_chunk(h, wg_ref, wu_ref, wd_ref, f0):
    gt = _dot(h, wg_ref[:, f0:f0 + FF_CHUNK])
    up = _dot(h, wu_ref[:, f0:f0 + FF_CHUNK])
    a = _bf(gt * jax.nn.sigmoid(gt) * up)
    return _dot(a, wd_ref[f0:f0 + FF_CHUNK, :])


def _swiglu(h, wg_ref, wu_ref, wd_ref):
    y = None
    for f0 in range(0, D_FF, FF_CHUNK):
        part = _swiglu_chunk(h, wg_ref, wu_ref, wd_ref, f0)
        y = part if y is None else y + part
    return y


class _WeightStream:
    def __init__(self, wg_hbm, wu_hbm, wd_hbm, wg_ref, wu_ref, wd_ref, sg_ref, su_ref, sd_ref, sems):
        self.hbm = (wg_hbm, wu_hbm, wd_hbm)
        self.dst = (wg_ref, wu_ref, wd_ref)
        self.stage = (sg_ref, su_ref, sd_ref)
        self.sems = sems
        gw = SWAP_GROUP * FF_CHUNK
        self.groups = [(c0, min(c0 + gw, D_FF)) for c0 in range(0, D_FF, gw)]

    def _copies(self, e, g):
        c0, c1 = self.groups[g]
        w, slot = c1 - c0, g % 2
        (wg_hbm, wu_hbm, wd_hbm), (sg_ref, su_ref, sd_ref) = self.hbm, self.stage
        sem = self.sems.at[slot]
        return (pltpu.make_async_copy(wg_hbm.at[e, :, pl.ds(c0, w)], sg_ref.at[slot, :, pl.ds(0, w)], sem),
                pltpu.make_async_copy(wu_hbm.at[e, :, pl.ds(c0, w)], su_ref.at[slot, :, pl.ds(0, w)], sem),
                pltpu.make_async_copy(wd_hbm.at[e, pl.ds(c0, w), :], sd_ref.at[slot, pl.ds(0, w), :], sem))

    def start(self, e, g):
        for cp in self._copies(e, g):
            cp.start()

    def install(self, e, g):
        for cp in self._copies(e, g):
            cp.wait()
        c0, c1 = self.groups[g]
        w, slot = c1 - c0, g % 2
        (wg_ref, wu_ref, wd_ref), (sg_ref, su_ref, sd_ref) = self.dst, self.stage
        wg_ref[:, c0:c1] = _bf(sg_ref[slot, :, 0:w])
        wu_ref[:, c0:c1] = _bf(su_ref[slot, :, 0:w])
        wd_ref[c0:c1, :] = _bf(sd_ref[slot, 0:w, :])

    def load_all(self, e):
        self.start(e, 0)
        for g in range(len(self.groups)):
            if g + 1 < len(self.groups):
                self.start(e, g + 1)
            self.install(e, g)


def _weight_scratch(d):
    gw = SWAP_GROUP * FF_CHUNK
    return [pltpu.VMEM((d, D_FF), jnp.bfloat16), pltpu.VMEM((d, D_FF), jnp.bfloat16),
            pltpu.VMEM((D_FF, d), jnp.bfloat16),
            pltpu.VMEM((2, d, gw), jnp.float32), pltpu.VMEM((2, d, gw), jnp.float32),
            pltpu.VMEM((2, gw, d), jnp.float32), pltpu.SemaphoreType.DMA((2,))]


def _ffn_dense_kernel(x_ref, ln_ref, fin_ref, wg_hbm, wu_hbm, wd_hbm, o_ref, wg_ref, wu_ref, wd_ref, sg_ref, su_ref,
                      sd_ref, sems, *, final_norm):
    @pl.when(pl.program_id(0) == 0)
    def _():
        _WeightStream(wg_hbm, wu_hbm, wd_hbm, wg_ref, wu_ref, wd_ref, sg_ref, su_ref, sd_ref, sems).load_all(0)

    x = x_ref[...]
    out = x + _swiglu(_bf(_rms(x, ln_ref[...])), wg_ref, wu_ref, wd_ref)
    if final_norm:
        out = _rms(out, fin_ref[...])
    o_ref[...] = out


def _ffn_dense_call(x2, ln_w, fin_w, wg, wu, wd, *, final_norm):
    n, d = x2.shape
    tm = min(ROW_TILE, n)
    assert n % tm == 0
    return pl.pallas_call(
        functools.partial(_ffn_dense_kernel, final_norm=final_norm),
        out_shape=jax.ShapeDtypeStruct(x2.shape, x2.dtype),
        grid=(n // tm,),
        in_specs=[
            pl.BlockSpec((tm, d), lambda i: (i, 0)),
            pl.BlockSpec((1, d), lambda i: (0, 0)),
            pl.BlockSpec((1, d), lambda i: (0, 0)),
            pl.BlockSpec(memory_space=pl.ANY),
            pl.BlockSpec(memory_space=pl.ANY),
            pl.BlockSpec(memory_space=pl.ANY),
        ],
        out_specs=pl.BlockSpec((tm, d), lambda i: (i, 0)),
        scratch_shapes=_weight_scratch(d),
        compiler_params=pltpu.CompilerParams(dimension_semantics=("arbitrary",), vmem_limit_bytes=VMEM_LIMIT),
        name="ffn_dense",
    )(x2, ln_w, fin_w, wg, wu, wd)


def _store_records(rec_ref, vals):
    m = vals.shape[0]
    for s in range(REC_ROWS):
        rec_ref[pl.ds(s, m, stride=REC_ROWS), :] = vals[:, s * LANES:(s + 1) * LANES]


def _load_records(rec_ref, m):
    return [rec_ref[pl.ds(s, m, stride=REC_ROWS), :] for s in range(REC_ROWS)]


def _copy_records(src_ref, src_off, dst_ref, dst_off, n, sem, max_rows, wait=False):
    bit = max_rows.bit_length() - 1
    while bit >= 0:
        size = (1 << bit) * REC_ROWS
        done = lax.shift_left(lax.shift_right_logical(n, bit + 1), bit + 1)
        src0 = 0 if src_off is None else pl.multiple_of((src_off + done) * REC_ROWS, REC_ROWS)
        dst0 = pl.multiple_of((dst_off + done) * REC_ROWS, REC_ROWS)

        @pl.when((lax.shift_right_logical(n, bit) & 1) == 1)
        def _(size=size, src0=src0, dst0=dst0):
            cp = pltpu.make_async_copy(src_ref.at[pl.ds(src0, size)], dst_ref.at[pl.ds(dst0, size)], sem)
            cp.wait() if wait else cp.start()
        bit -= 1


def _one_hot_rows(pos0, pos1, n_rows):
    r = lax.broadcasted_iota(jnp.int32, (n_rows, pos0.shape[1]), 0)
    return jnp.where((r == pos0) | (r == pos1), 1.0, 0.0).astype(jnp.bfloat16)


def _route_kernel(x_ref, ln_ref, wrh_ref, wrl_ref, pos_ref, gate_ref, cnt_ref):
    ts = x_ref.shape[0]
    t = pl.program_id(0)
    h = _rms(x_ref[...], ln_ref[...])
    h_hi = _bf(h)
    h_lo = _bf(h - h_hi.astype(jnp.float32))
    logits = _dot(h_hi, wrh_ref[...]) + (_dot(h_hi, wrl_ref[...]) + _dot(h_lo, wrh_ref[...]))
    lt = jnp.transpose(logits)[0:N_EXPERTS, :]
    row = lax.broadcasted_iota(jnp.int32, lt.shape, 0)
    neg = jnp.float32(-jnp.inf)
    m1 = jnp.max(lt, axis=0, keepdims=True)
    i1 = jnp.min(jnp.where(lt == m1, row, N_EXPERTS), axis=0, keepdims=True)
    rest = jnp.where(row == i1, neg, lt)
    m2 = jnp.max(rest, axis=0, keepdims=True)
    i2 = jnp.min(jnp.where(rest == m2, row, N_EXPERTS), axis=0, keepdims=True)
    e2 = jnp.exp(m2 - m1)
    den = 1.0 + e2
    g1 = 1.0 / den
    g2 = e2 / den
    sel1 = row == i1
    sel2 = row == i2
    oh = jnp.where(sel1 | sel2, 1.0, 0.0)

    sp = lax.broadcasted_iota(jnp.int32, (ts, ts), 0)
    sc = lax.broadcasted_iota(jnp.int32, (ts, ts), 1)
    upper = jnp.where(sp < sc, 1.0, 0.0).astype(jnp.bfloat16)
    rank = _dot(_bf(oh), upper)
    rk1 = jnp.sum(jnp.where(sel1, rank, 0.0), axis=0, keepdims=True)
    rk2 = jnp.sum(jnp.where(sel2, rank, 0.0), axis=0, keepdims=True)

    cnts, offs = [], []
    off = jnp.int32(0)
    for e in range(N_EXPERTS):
        c = jnp.sum(oh[e:e + 1, :]).astype(jnp.int32)
        cnts.append(c)
        offs.append(off)
        off = off + c
    off1 = jnp.zeros_like(rk1)
    off2 = jnp.zeros_like(rk2)
    for e in range(N_EXPERTS):
        fe = offs[e].astype(jnp.float32)
        off1 = jnp.where(i1 == e, fe, off1)
        off2 = jnp.where(i2 == e, fe, off2)
    pos1 = (off1 + rk1).astype(jnp.int32)
    pos2 = (off2 + rk2).astype(jnp.int32)
    pos_ref[0] = jnp.concatenate([pos1, pos2, jnp.zeros((SUBLANES - TOP_K, ts), jnp.int32)], axis=0)
    gate_ref[0] = jnp.concatenate([g1, g2, jnp.zeros((SUBLANES - TOP_K, ts), jnp.float32)], axis=0)
    for e in range(N_EXPERTS):
        cnt_ref[t * N_EXPERTS + e] = cnts[e]


def _route_call(x2, ln_w, wr_hi, wr_lo):
    n, d = x2.shape
    ts = min(ROW_TILE, n)
    nt = n // ts
    return pl.pallas_call(
        _route_kernel,
        out_shape=(
            jax.ShapeDtypeStruct((nt, SUBLANES, ts), jnp.int32),
            jax.ShapeDtypeStruct((nt, SUBLANES, ts), jnp.float32),
            jax.ShapeDtypeStruct((nt * N_EXPERTS,), jnp.int32),
        ),
        grid=(nt,),
        in_specs=[
            pl.BlockSpec((ts, d), lambda i: (i, 0)),
            pl.BlockSpec((1, d), lambda i: (0, 0)),
            pl.BlockSpec((d, LANES), lambda i: (0, 0)),
            pl.BlockSpec((d, LANES), lambda i: (0, 0)),
        ],
        out_specs=(
            pl.BlockSpec((1, SUBLANES, ts), lambda i: (i, 0, 0)),
            pl.BlockSpec((1, SUBLANES, ts), lambda i: (i, 0, 0)),
            pl.BlockSpec(memory_space=pltpu.SMEM),
        ),
        compiler_params=pltpu.CompilerParams(dimension_semantics=("arbitrary",), vmem_limit_bytes=VMEM_LIMIT),
        name="route",
    )(x2, ln_w, wr_hi, wr_lo)


def _dispatch_kernel(seg_ref, cnt_ref, zstart_ref, zlen_ref, nval_ref, x_ref, ln_ref, pos_ref, xs_hbm,
                     stage_ref, zero_ref, sem, zsem, *, max_tiles):
    ts = x_ref.shape[0]
    n_sorted = TOP_K * ts
    t = pl.program_id(0)
    nt = pl.num_programs(0)
    h = _bf(_rms(x_ref[...], ln_ref[...]))
    pos = pos_ref[0]
    perm = _one_hot_rows(pos[0:1, :], pos[1:2, :], n_sorted)
    sorted_h = _dot(perm, h)

    @pl.when(t > 0)
    def _():
        pltpu.make_async_copy(stage_ref, xs_hbm.at[pl.ds(0, n_sorted * REC_ROWS)], sem).wait()

    _store_records(stage_ref, sorted_h)

    off = jnp.int32(0)
    for e in range(N_EXPERTS):
        c = cnt_ref[t * N_EXPERTS + e]
        _copy_records(stage_ref, off, xs_hbm, seg_ref[t * N_EXPERTS + e], c, sem, ts)
        off = off + c

    @pl.when(t == nt - 1)
    def _():
        pltpu.make_async_copy(stage_ref, xs_hbm.at[pl.ds(0, n_sorted * REC_ROWS)], sem).wait()
        tm = zero_ref.shape[0] // REC_ROWS
        zero_ref[...] = jnp.zeros_like(zero_ref)

        def tail_fill(k):
            return pltpu.make_async_copy(
                zero_ref, xs_hbm.at[pl.ds(pl.multiple_of((nval_ref[0] + k) * tm * REC_ROWS, REC_ROWS),
                                          tm * REC_ROWS)], zsem)

        for wait in (False, True):
            for e in range(N_EXPERTS):
                _copy_records(zero_ref, None, xs_hbm, zstart_ref[e], zlen_ref[e], zsem, tm, wait=wait)
            for k in range(N_EXPERTS):
                @pl.when(nval_ref[0] + k < max_tiles)
                def _(k=k, wait=wait):
                    tail_fill(k).wait() if wait else tail_fill(k).start()


def _dispatch_call(seg, cnt, zstart, zlen, n_valid, x2, ln_w, pos, max_tiles):
    n, d = x2.shape
    ts = min(ROW_TILE, n)
    nt = n // ts
    return pl.pallas_call(
        functools.partial(_dispatch_kernel, max_tiles=max_tiles),
        out_shape=jax.ShapeDtypeStruct((max_tiles * ROW_TILE * REC_ROWS, LANES), jnp.float32),
        grid_spec=pltpu.PrefetchScalarGridSpec(
            num_scalar_prefetch=5,
            grid=(nt,),
            in_specs=[
                pl.BlockSpec((ts, d), lambda i, *_: (i, 0)),
                pl.BlockSpec((1, d), lambda i, *_: (0, 0)),
                pl.BlockSpec((1, SUBLANES, ts), lambda i, *_: (i, 0, 0)),
            ],
            out_specs=pl.BlockSpec(memory_space=pl.ANY),
            scratch_shapes=[
                pltpu.VMEM((TOP_K * ts * REC_ROWS, LANES), jnp.float32),
                pltpu.VMEM((ROW_TILE * REC_ROWS, LANES), jnp.float32),
                pltpu.SemaphoreType.DMA(()),
                pltpu.SemaphoreType.DMA(()),
            ],
        ),
        compiler_params=pltpu.CompilerParams(dimension_semantics=("arbitrary",), vmem_limit_bytes=VMEM_LIMIT),
        name="dispatch",
    )(seg, cnt, zstart, zlen, n_valid, x2, ln_w, pos)


def _ffn_group_kernel(texp_ref, tnext_ref, tswap_ref, nval_ref, xs_ref, wg_hbm, wu_hbm, wd_hbm, ys_ref,
                      wg_ref, wu_ref, wd_ref, sg_ref, su_ref, sd_ref, sems):
    j = pl.program_id(0)
    n_chunks = D_FF // FF_CHUNK
    ws = _WeightStream(wg_hbm, wu_hbm, wd_hbm, wg_ref, wu_ref, wd_ref, sg_ref, su_ref, sd_ref, sems)
    groups, start, install = ws.groups, ws.start, ws.install

    def load_h():
        tm = xs_ref.shape[0] // REC_ROWS
        return jnp.concatenate([_bf(w) for w in _load_records(xs_ref, tm)], axis=1)

    @pl.when(j == 0)
    def _():
        ws.load_all(texp_ref[0])

    valid = j < nval_ref[0]
    swap = tswap_ref[j] == 1
    not_ = jnp.logical_not

    @pl.when(not_(valid))
    def _():
        ys_ref[...] = jnp.zeros_like(ys_ref)

    @pl.when(valid & not_(swap))
    def _():
        _store_records(ys_ref, _swiglu(load_h(), wg_ref, wu_ref, wd_ref))

    @pl.when(valid & swap)
    def _():
        e = tnext_ref[j]
        start(e, 0)
        start(e, 1)
        h = load_h()
        y = None
        for f in range(n_chunks):
            part = _swiglu_chunk(h, wg_ref, wu_ref, wd_ref, f * FF_CHUNK)
            y = part if y is None else y + part
            g = f // SWAP_GROUP
            if (f + 1) * FF_CHUNK == groups[g][1]:
                install(e, g)
                if g + 2 < len(groups):
                    start(e, g + 2)
        _store_records(ys_ref, y)


def _ffn_group_call(tile_expert, tile_next, tile_swap, n_valid, xs, wg, wu, wd):
    rows = xs.shape[0]
    tm = ROW_TILE
    n_tiles = tile_expert.shape[0]
    d = wg.shape[1]
    return pl.pallas_call(
        _ffn_group_kernel,
        out_shape=jax.ShapeDtypeStruct((rows, LANES), jnp.float32),
        grid_spec=pltpu.PrefetchScalarGridSpec(
            num_scalar_prefetch=4,
            grid=(n_tiles,),
            in_specs=[
                pl.BlockSpec((tm * REC_ROWS, LANES), lambda j, *_: (j, 0)),
                pl.BlockSpec(memory_space=pl.ANY),
                pl.BlockSpec(memory_space=pl.ANY),
                pl.BlockSpec(memory_space=pl.ANY),
            ],
            out_specs=pl.BlockSpec((tm * REC_ROWS, LANES), lambda j, *_: (j, 0)),
            scratch_shapes=_weight_scratch(d),
        ),
        compiler_params=pltpu.CompilerParams(dimension_semantics=("arbitrary",), vmem_limit_bytes=VMEM_LIMIT),
        name="ffn_group",
    )(tile_expert, tile_next, tile_swap, n_valid, xs, wg, wu, wd)


def _combine_kernel(seg_ref, cnt_ref, x_ref, fin_ref, pos_ref, gate_ref, ys_hbm, o_ref, ybuf_ref, sems,
                    *, final_norm):
    ts = x_ref.shape[0]
    n_sorted = TOP_K * ts
    t = pl.program_id(0)
    nt = pl.num_programs(0)

    def fetch(tile, slot):
        off = jnp.int32(0)
        for e in range(N_EXPERTS):
            c = cnt_ref[tile * N_EXPERTS + e]
            _copy_records(ys_hbm, seg_ref[tile * N_EXPERTS + e], ybuf_ref.at[slot], off, c, sems.at[slot], ts)
            off = off + c

    @pl.when(t == 0)
    def _():
        fetch(0, 0)

    @pl.when(t + 1 < nt)
    def _():
        fetch(t + 1, (t + 1) % 2)

    slot = t % 2
    pltpu.make_async_copy(ys_hbm.at[pl.ds(0, n_sorted * REC_ROWS)], ybuf_ref.at[slot], sems.at[slot]).wait()
    ys = jnp.concatenate([_bf(w) for w in _load_records(ybuf_ref.at[slot], n_sorted)], axis=1)
    pos = pos_ref[0]
    gates = gate_ref[0]
    r = lax.broadcasted_iota(jnp.int32, (n_sorted, ts), 0)
    sel = _bf(jnp.where(r == pos[0:1, :], gates[0:1, :], 0.0) + jnp.where(r == pos[1:2, :], gates[1:2, :], 0.0))
    y = lax.dot_general(sel, ys, (((0,), (0,)), ((), ())), preferred_element_type=jnp.float32)
    out = x_ref[...] + y
    if final_norm:
        out = _rms(out, fin_ref[...])
    o_ref[...] = out


def _combine_call(seg, cnt, x2, fin_w, pos, gates, ys, *, final_norm):
    n, d = x2.shape
    ts = min(ROW_TILE, n)
    nt = n // ts
    return pl.pallas_call(
        functools.partial(_combine_kernel, final_norm=final_norm),
        out_shape=jax.ShapeDtypeStruct(x2.shape, x2.dtype),
        grid_spec=pltpu.PrefetchScalarGridSpec(
            num_scalar_prefetch=2,
            grid=(nt,),
            in_specs=[
                pl.BlockSpec((ts, d), lambda i, sg, ct: (i, 0)),
                pl.BlockSpec((1, d), lambda i, sg, ct: (0, 0)),
                pl.BlockSpec((1, SUBLANES, ts), lambda i, sg, ct: (i, 0, 0)),
                pl.BlockSpec((1, SUBLANES, ts), lambda i, sg, ct: (i, 0, 0)),
                pl.BlockSpec(memory_space=pl.ANY),
            ],
            out_specs=pl.BlockSpec((ts, d), lambda i, sg, ct: (i, 0)),
            scratch_shapes=[
                pltpu.VMEM((2, TOP_K * ts * REC_ROWS, LANES), jnp.float32),
                pltpu.SemaphoreType.DMA((2,)),
            ],
        ),
        compiler_params=pltpu.CompilerParams(dimension_semantics=("arbitrary",), vmem_limit_bytes=VMEM_LIMIT),
        name="combine",
    )(seg, cnt, x2, fin_w, pos, gates, ys)


def _moe(x2, ln_w, fin_w, w_router, wg, wu, wd, *, final_norm):
    n, d = x2.shape
    tm = ROW_TILE
    wr_p = jnp.concatenate([w_router, jnp.zeros((d, LANES - N_EXPERTS), w_router.dtype)], axis=1)
    wr_hi = _bf(wr_p)
    wr_lo = _bf(wr_p - wr_hi.astype(jnp.float32))
    pos, gates, cnt = _route_call(x2, ln_w, wr_hi, wr_lo)

    i32 = jnp.int32
    nt = cnt.shape[0] // N_EXPERTS
    cnt2 = cnt.reshape(nt, N_EXPERTS)
    totals = jnp.sum(cnt2, axis=0)
    tiles_e = (totals + tm - 1) // tm
    ends = jnp.cumsum(tiles_e)
    base = (ends - tiles_e) * tm
    seg = (base[None, :] + jnp.cumsum(cnt2, axis=0) - cnt2).reshape(-1).astype(i32)
    zstart = (base + totals).astype(i32)
    zlen = (tiles_e * tm - totals).astype(i32)
    n_valid = ends[-1].reshape(1).astype(i32)
    max_tiles = (TOP_K * n) // tm + N_EXPERTS
    j = jnp.minimum(jnp.arange(max_tiles, dtype=i32), n_valid - 1)
    tile_expert = jnp.sum((j[:, None] >= ends[None, :]).astype(i32), axis=1).astype(i32)
    tile_next = jnp.concatenate([tile_expert[1:], tile_expert[-1:]])
    tile_swap = (tile_next != tile_expert).astype(i32)

    xs = _dispatch_call(seg, cnt, zstart, zlen, n_valid, x2, ln_w, pos, max_tiles)
    ys = _ffn_group_call(tile_expert, tile_next, tile_swap, n_valid, xs, wg, wu, wd)
    return _combine_call(seg, cnt, x2, fin_w, pos, gates, ys, final_norm=final_norm)


def _prep_w_in(w):
    depth, d, _ = w.shape
    o_gr = 2 * GLA_KEY + 2 * GLA_VAL
    parts = [w[:, :, :o_gr], w[:, :, o_gr + GATE_RANK:], w[:, :, o_gr:o_gr + GATE_RANK],
             jnp.zeros((depth, d, GATE_PAD - GATE_RANK), w.dtype)]
    return _bf(jnp.concatenate(parts, axis=2))


def kernel(x, ln1_w, w_in, w_gate_up, b_gate, gla_norm_w, conv_w, conv_b, cn_w, cn_b, w_out, ln2_w, wd_gate,
           wd_up, wd_down, w_router, we_gate, we_up, we_down, final_norm_w):
    bsz, seq, d = x.shape
    depth = ln1_w.shape[0]
    n = bsz * seq
    row = lambda a: a.reshape(1, -1)
    rows = lambda a: a[:, None, :]
    w_in_p = _prep_w_in(w_in)
    wgu_p = _bf(jnp.concatenate(
        [w_gate_up, jnp.zeros((depth, GATE_PAD - GATE_RANK, GLA_KEY), w_gate_up.dtype)], axis=1))
    cw_p = jnp.concatenate([conv_w, jnp.zeros((depth, CONV_HALO - CONV_WIDTH, CONV_CH), conv_w.dtype)], axis=1)
    w_out_b = _bf(w_out)
    for l in range(depth):
        x = _mixer_call(x, l, rows(ln1_w), w_in_p, wgu_p, rows(b_gate), rows(gla_norm_w), cw_p, rows(conv_b),
                        rows(cn_w), rows(cn_b), w_out_b)
        x2 = x.reshape(n, d)
        last = l == depth - 1
        i = l // 2
        if l % 2 == 0:
            x2 = _ffn_dense_call(x2, row(ln2_w[l]), row(final_norm_w), wd_gate[i:i + 1], wd_up[i:i + 1],
                                 wd_down[i:i + 1], final_norm=last)
        else:
            x2 = _moe(x2, row(ln2_w[l]), row(final_norm_w), w_router[i], we_gate[i], we_up[i], we_down[i],
                      final_norm=last)
        x = x2.reshape(bsz, seq, d)
    return x
```

```python
import functools

import jax
import jax.numpy as jnp
from jax import lax
from jax.experimental import pallas as pl
from jax.experimental.pallas import tpu as pltpu

D_MODEL = 1024
GLA_HEADS = 4
GLA_DK = 64
GLA_DV = 128
GLA_KEY = GLA_HEADS * GLA_DK
GLA_VAL = GLA_HEADS * GLA_DV
GATE_RANK = 16
GATE_NORMALIZER = 16.0
CONV_CH = 512
CONV_WIDTH = 31
D_FF = 2816
N_EXPERTS = 8
TOP_K = 2
EPS = 1e-6

LANES = 128
CHUNK = 64
TIME_TILE = 512
ROW_TILE = 512
FF_CHUNK = 256
SWAP_GROUP = 3
CONV_HALO = 32
GATE_PAD = LANES
IN_COLS_A = 2 * GLA_KEY + 2 * GLA_VAL
IN_COLS_B = 2 * CONV_CH + GATE_PAD
VMEM_LIMIT = 56 * 1024 * 1024
SUBLANES = 8
REC_ROWS = D_MODEL // LANES

_OQ, _OK, _OV, _OG = 0, GLA_KEY, 2 * GLA_KEY, 2 * GLA_KEY + GLA_VAL
_OCA, _OCB, _OGR = 0, CONV_CH, 2 * CONV_CH

_LEVELS = (32, 16, 8, 4, 2, 1)


def _rms(x, w):
    return x * lax.rsqrt(jnp.mean(x * x, axis=-1, keepdims=True) + EPS) * w


def _bf(x):
    return x.astype(jnp.bfloat16)


def _dot(a, b):
    return jnp.dot(a, b, preferred_element_type=jnp.float32)


def _boundary(b, hs, rows):
    n, c = b.shape
    blk = 2 * hs
    if blk >= 8:
        b3 = b.reshape(n // blk, blk, c)
        return jnp.broadcast_to(b3[:, hs - 1:hs, :], (n // blk, blk, c)).reshape(n, c)
    y = pltpu.roll(b, n - (hs - 1), 0) if hs > 1 else b
    s = 1
    while s < blk:
        y = jnp.where((rows & s) != 0, pltpu.roll(y, s, 0), y)
        s *= 2
    return y


def _mixer_kernel(x_ref, ln_ref, wina_ref, winb_ref, wgu_ref, bg_ref, gnw_ref, cw_ref, cb_ref, cnw_ref, cnb_ref,
                  wout_ref, o_ref, s_ref, ubuf_ref, ush_ref, q_ref, y_ref):
    tt = x_ref.shape[1]
    nch = tt // CHUNK
    t = pl.program_id(1)

    @pl.when(t == 0)
    def _():
        s_ref[...] = jnp.zeros_like(s_ref)
        ubuf_ref[0:CONV_HALO, :] = jnp.zeros((CONV_HALO, CONV_CH), jnp.float32)
        ubuf_ref[CONV_HALO + tt:, :] = jnp.zeros((SUBLANES, CONV_CH), jnp.float32)

    x = x_ref[0]
    h = _bf(_rms(x, ln_ref[...]))
    za = _dot(h, wina_ref[...])
    zb = _dot(h, winb_ref[...])

    q = za[:, _OQ:_OQ + GLA_KEY] * (GLA_DK ** -0.5)
    k = za[:, _OK:_OK + GLA_KEY]
    v = za[:, _OV:_OV + GLA_VAL]
    g = za[:, _OG:_OG + GLA_VAL]
    ca = zb[:, _OCA:_OCA + CONV_CH]
    cb = zb[:, _OCB:_OCB + CONV_CH]
    gr = zb[:, _OGR:_OGR + GATE_PAD]

    ubuf_ref[CONV_HALO:CONV_HALO + tt, :] = ca * jax.nn.sigmoid(cb)
    off0 = CONV_HALO - (CONV_WIDTH - 1)
    half = SUBLANES // 2
    ufull = ubuf_ref[...]
    n_u = ufull.shape[0]
    for r in range(1, half):
        ush_ref[r - 1] = pltpu.roll(ufull, n_u - r, 0)[0:n_u - SUBLANES, :]

    def taps(rows0, n_rows, c0, acc, upper):
        for j in range(CONV_WIDTH):
            r = (off0 + j) % SUBLANES
            if (r >= half) != upper:
                continue
            a0 = rows0 + off0 + j - r
            src = ubuf_ref if r % half == 0 else ush_ref.at[r % half - 1]
            acc = acc + cw_ref[j:j + 1, c0:c0 + LANES] * src[a0:a0 + n_rows, c0:c0 + LANES]
        return acc

    for c0 in range(0, CONV_CH, LANES):
        for r0 in list(range(0, tt, CHUNK)) + [tt]:
            n_rows = CHUNK if r0 < tt else SUBLANES
            q_ref[r0:r0 + n_rows, c0:c0 + LANES] = taps(r0, n_rows, c0, jnp.zeros((n_rows, LANES), jnp.float32),
                                                        True)
    for r0 in range(0, tt, CHUNK):
        for c0 in range(0, CONV_CH, LANES):
            acc = jnp.broadcast_to(cb_ref[:, c0:c0 + LANES], (CHUNK, LANES))
            acc = taps(r0, CHUNK, c0, acc, False)
            y_ref[r0:r0 + CHUNK, c0:c0 + LANES] = acc + q_ref[r0 + half:r0 + half + CHUNK, c0:c0 + LANES]
    ubuf_ref[0:CONV_HALO, :] = ubuf_ref[tt:tt + CONV_HALO, :]
    yc = y_ref[...]
    mu = jnp.mean(yc, axis=-1, keepdims=True)
    yd = yc - mu
    var = jnp.mean(yd * yd, axis=-1, keepdims=True)
    u = yd * lax.rsqrt(var + EPS) * cnw_ref[...] + cnb_ref[...]
    u = u * jax.nn.sigmoid(u)

    logit = _dot(_bf(gr), wgu_ref[...]) + bg_ref[...]
    la = jax.nn.log_sigmoid(logit) * (1.0 / GATE_NORMALIZER)
    rows = lax.broadcasted_iota(jnp.int32, (tt, GLA_KEY), 0)
    la_hi = _bf(la)
    la_r = la - la_hi.astype(jnp.float32)
    la_mid = _bf(la_r)
    la_lo = _bf(la_r - la_mid.astype(jnp.float32))
    tri = (lax.broadcasted_iota(jnp.int32, (CHUNK, CHUNK), 1)
           <= lax.broadcasted_iota(jnp.int32, (CHUNK, CHUNK), 0)).astype(jnp.bfloat16)
    b = jnp.concatenate(
        [_dot(tri, la_hi[c * CHUNK:(c + 1) * CHUNK]) + (_dot(tri, la_mid[c * CHUNK:(c + 1) * CHUNK])
                                                        + _dot(tri, la_lo[c * CHUNK:(c + 1) * CHUNK]))
         for c in range(nch)], axis=0)
    b3 = b.reshape(nch, CHUNK, GLA_KEY)
    blast3 = b3[:, CHUNK - 1:CHUNK, :]
    blast = jnp.broadcast_to(blast3, (nch, CHUNK, GLA_KEY)).reshape(tt, GLA_KEY)
    qe = _bf(q * jnp.exp(b))
    kl = _bf(k * jnp.exp(blast - b))
    vb = _bf(v)
    dl = jnp.exp(blast3.reshape(nch, GLA_KEY))
    dl_t = jnp.transpose(jnp.concatenate([dl, jnp.zeros((LANES - nch, GLA_KEY), jnp.float32)], axis=0))

    qh, kh = [_bf(q)], [_bf(k)]
    for hs in _LEVELS:
        e = jnp.exp(-jnp.abs(b - _boundary(b, hs, rows)))
        qh.append(_bf(q * e))
        kh.append(_bf(k * e))

    ii = lax.broadcasted_iota(jnp.int32, (CHUNK, GLA_HEADS * CHUNK), 0)
    jj = lax.broadcasted_iota(jnp.int32, (CHUNK, GLA_HEADS * CHUNK), 1) & (CHUNK - 1)
    masks = [ii == jj]
    for hs in _LEVELS:
        blk = 2 * hs
        masks.append(((ii // blk) == (jj // blk)) & ((ii & (blk - 1)) >= hs) & ((jj & (blk - 1)) < hs))
    rk = lax.broadcasted_iota(jnp.int32, (GLA_HEADS * CHUNK, GLA_KEY), 0) // CHUNK
    ck = lax.broadcasted_iota(jnp.int32, (GLA_HEADS * CHUNK, GLA_KEY), 1) // GLA_DK
    bd_k = (rk == ck).astype(jnp.bfloat16)
    rv = lax.broadcasted_iota(jnp.int32, (GLA_HEADS * CHUNK, GLA_VAL), 0) // CHUNK
    cv = lax.broadcasted_iota(jnp.int32, (GLA_HEADS * CHUNK, GLA_VAL), 1) // GLA_DV
    bd_v = rv == cv
    bd_vb = bd_v.astype(jnp.bfloat16)

    st = s_ref[...]
    o_parts = []
    for c in range(nch):
        sl = slice(c * CHUNK, (c + 1) * CHUNK)
        att = jnp.zeros((CHUNK, GLA_HEADS * CHUNK), jnp.float32)
        for lvl in range(len(masks)):
            kbd = jnp.concatenate([kh[lvl][sl]] * GLA_HEADS, axis=0) * bd_k
            sc = lax.dot_general(qh[lvl][sl], kbd, (((1,), (1,)), ((), ())),
                                 preferred_element_type=jnp.float32)
            att = jnp.where(masks[lvl], sc, att)
        vbd = jnp.concatenate([vb[sl]] * GLA_HEADS, axis=0) * bd_vb
        o_parts.append(_dot(_bf(att), vbd) + _dot(qe[sl], _bf(st)))
        upd = lax.dot_general(kl[sl], vb[sl], (((0,), (0,)), ((), ())), preferred_element_type=jnp.float32)
        st = st * dl_t[:, c:c + 1] + jnp.where(bd_v, upd, 0.0)
    s_ref[...] = st
    o = jnp.concatenate(o_parts, axis=0)

    gnw = gnw_ref[...]
    heads = []
    for hd in range(GLA_HEADS):
        oh = o[:, hd * GLA_DV:(hd + 1) * GLA_DV]
        heads.append(_rms(oh, gnw))
    o = jnp.concatenate(heads, axis=-1) * (g * jax.nn.sigmoid(g))

    mix = _bf(jnp.concatenate([o, u], axis=-1))
    o_ref[0] = x + _dot(mix, wout_ref[...])


def _mixer_call(x, layer, ln_w, w_in_a, w_in_b, wgu_p, b_gate, gnw, conv_w, conv_b, cn_w, cn_b, w_out_b):
    bsz, seq, d = x.shape
    tt = min(TIME_TILE, seq)
    assert seq % tt == 0 and tt % CHUNK == 0
    const = lambda *shape: pl.BlockSpec((None,) + shape, lambda b, t: (layer,) + (0,) * len(shape))
    return pl.pallas_call(
        _mixer_kernel,
        out_shape=jax.ShapeDtypeStruct(x.shape, x.dtype),
        grid=(bsz, seq // tt),
        in_specs=[
            pl.BlockSpec((1, tt, d), lambda b, t: (b, t, 0)),
            const(1, d), const(d, IN_COLS_A), const(d, IN_COLS_B), const(GATE_PAD, GLA_KEY), const(1, GLA_KEY),
            const(1, GLA_DV),
            const(CONV_HALO, CONV_CH), const(1, CONV_CH), const(1, CONV_CH), const(1, CONV_CH),
            const(GLA_VAL + CONV_CH, d),
        ],
        out_specs=pl.BlockSpec((1, tt, d), lambda b, t: (b, t, 0)),
        scratch_shapes=[
            pltpu.VMEM((GLA_HEADS * GLA_DK, GLA_VAL), jnp.float32),
            pltpu.VMEM((CONV_HALO + tt + SUBLANES, CONV_CH), jnp.float32),
            pltpu.VMEM((SUBLANES // 2 - 1, CONV_HALO + tt, CONV_CH), jnp.float32),
            pltpu.VMEM((tt + SUBLANES, CONV_CH), jnp.float32),
            pltpu.VMEM((tt, CONV_CH), jnp.float32),
        ],
        compiler_params=pltpu.CompilerParams(dimension_semantics=("arbitrary", "arbitrary"),
                                             vmem_limit_bytes=VMEM_LIMIT),
        name="mixer",
    )(x, ln_w, w_in_a, w_in_b, wgu_p, b_gate, gnw, conv_w, conv_b, cn_w, cn_b, w_out_b)


def _swiglu_chunk(h, wg_ref, wu_ref, wd_ref, f0):
    gt = _dot(h, wg_ref[:, f0:f0 + FF_CHUNK])
    up = _dot(h, wu_ref[:, f0:f0 + FF_CHUNK])
    a = _bf(gt * jax.nn.sigmoid(gt) * up)
    return _dot(a, wd_ref[f0:f0 + FF_CHUNK, :])


def _swiglu(h, wg_ref, wu_ref, wd_ref):
    y = None
    for f0 in range(0, D_FF, FF_CHUNK):
        part = _swiglu_chunk(h, wg_ref, wu_ref, wd_ref, f0)
        y = part if y is None else y + part
    return y


class _WeightStream:
    def __init__(self, wg_hbm, wu_hbm, wd_hbm, wg_ref, wu_ref, wd_ref, sg_ref, su_ref, sd_ref, sems):
        self.hbm = (wg_hbm, wu_hbm, wd_hbm)
        self.dst = (wg_ref, wu_ref, wd_ref)
        self.stage = (sg_ref, su_ref, sd_ref)
        self.sems = sems
        gw = SWAP_GROUP * FF_CHUNK
        self.groups = [(c0, min(c0 + gw, D_FF)) for c0 in range(0, D_FF, gw)]

    def _copies(self, e, g):
        c0, c1 = self.groups[g]
        w, slot = c1 - c0, g % 2
        (wg_hbm, wu_hbm, wd_hbm), (sg_ref, su_ref, sd_ref) = self.hbm, self.stage
        sem = self.sems.at[slot]
        return (pltpu.make_async_copy(wg_hbm.at[e, :, pl.ds(c0, w)], sg_ref.at[slot, :, pl.ds(0, w)], sem),
                pltpu.make_async_copy(wu_hbm.at[e, :, pl.ds(c0, w)], su_ref.at[slot, :, pl.ds(0, w)], sem),
                pltpu.make_async_copy(wd_hbm.at[e, pl.ds(c0, w), :], sd_ref.at[slot, pl.ds(0, w), :], sem))

    def start(self, e, g):
        for cp in self._copies(e, g):
            cp.start()

    def install(self, e, g):
        for cp in self._copies(e, g):
            cp.wait()
        c0, c1 = self.groups[g]
        w, slot = c1 - c0, g % 2
        (wg_ref, wu_ref, wd_ref), (sg_ref, su_ref, sd_ref) = self.dst, self.stage
        wg_ref[:, c0:c1] = _bf(sg_ref[slot, :, 0:w])
        wu_ref[:, c0:c1] = _bf(su_ref[slot, :, 0:w])
        wd_ref[c0:c1, :] = _bf(sd_ref[slot, 0:w, :])

    def load_all(self, e):
        self.start(e, 0)
        for g in range(len(self.groups)):
            if g + 1 < len(self.groups):
                self.start(e, g + 1)
            self.install(e, g)


def _weight_scratch(d):
    gw = SWAP_GROUP * FF_CHUNK
    return [pltpu.VMEM((d, D_FF), jnp.bfloat16), pltpu.VMEM((d, D_FF), jnp.bfloat16),
            pltpu.VMEM((D_FF, d), jnp.bfloat16),
            pltpu.VMEM((2, d, gw), jnp.float32), pltpu.VMEM((2, d, gw), jnp.float32),
            pltpu.VMEM((2, gw, d), jnp.float32), pltpu.SemaphoreType.DMA((2,))]


def _ffn_dense_kernel(x_ref, ln_ref, fin_ref, wg_hbm, wu_hbm, wd_hbm, o_ref, wg_ref, wu_ref, wd_ref, sg_ref, su_ref,
                      sd_ref, sems, *, final_norm):
    @pl.when(pl.program_id(0) == 0)
    def _():
        _WeightStream(wg_hbm, wu_hbm, wd_hbm, wg_ref, wu_ref, wd_ref, sg_ref, su_ref, sd_ref, sems).load_all(0)

    x = x_ref[...]
    out = x + _swiglu(_bf(_rms(x, ln_ref[...])), wg_ref, wu_ref, wd_ref)
    if final_norm:
        out = _rms(out, fin_ref[...])
    o_ref[...] = out


def _ffn_dense_call(x2, ln_w, fin_w, wg, wu, wd, *, final_norm):
    n, d = x2.shape
    tm = min(ROW_TILE, n)
    assert n % tm == 0
    return pl.pallas_call(
        functools.partial(_ffn_dense_kernel, final_norm=final_norm),
        out_shape=jax.ShapeDtypeStruct(x2.shape, x2.dtype),
        grid=(n // tm,),
        in_specs=[
            pl.BlockSpec((tm, d), lambda i: (i, 0)),
            pl.BlockSpec((1, d), lambda i: (0, 0)),
            pl.BlockSpec((1, d), lambda i: (0, 0)),
            pl.BlockSpec(memory_space=pl.ANY),
            pl.BlockSpec(memory_space=pl.ANY),
            pl.BlockSpec(memory_space=pl.ANY),
        ],
        out_specs=pl.BlockSpec((tm, d), lambda i: (i, 0)),
        scratch_shapes=_weight_scratch(d),
        compiler_params=pltpu.CompilerParams(dimension_semantics=("arbitrary",), vmem_limit_bytes=VMEM_LIMIT),
        name="ffn_dense",
    )(x2, ln_w, fin_w, wg, wu, wd)


def _store_records(rec_ref, vals):
    m = vals.shape[0]
    for s in range(REC_ROWS):
        rec_ref[pl.ds(s, m, stride=REC_ROWS), :] = vals[:, s * LANES:(s + 1) * LANES]


def _load_records(rec_ref, m):
    return [rec_ref[pl.ds(s, m, stride=REC_ROWS), :] for s in range(REC_ROWS)]


def _copy_records(src_ref, src_off, dst_ref, dst_off, n, sem, max_rows, wait=False):
    bit = max_rows.bit_length() - 1
    while bit >= 0:
        size = (1 << bit) * REC_ROWS
        done = lax.shift_left(lax.shift_right_logical(n, bit + 1), bit + 1)
        src0 = 0 if src_off is None else pl.multiple_of((src_off + done) * REC_ROWS, REC_ROWS)
        dst0 = pl.multiple_of((dst_off + done) * REC_ROWS, REC_ROWS)

        @pl.when((lax.shift_right_logical(n, bit) & 1) == 1)
        def _(size=size, src0=src0, dst0=dst0):
            cp = pltpu.make_async_copy(src_ref.at[pl.ds(src0, size)], dst_ref.at[pl.ds(dst0, size)], sem)
            cp.wait() if wait else cp.start()
        bit -= 1


def _one_hot_rows(pos0, pos1, n_rows):
    r = lax.broadcasted_iota(jnp.int32, (n_rows, pos0.shape[1]), 0)
    return jnp.where((r == pos0) | (r == pos1), 1.0, 0.0).astype(jnp.bfloat16)


def _route_kernel(x_ref, ln_ref, wrh_ref, wrl_ref, pos_ref, gate_ref, cnt_ref):
    ts = x_ref.shape[0]
    t = pl.program_id(0)
    h = _rms(x_ref[...], ln_ref[...])
    h_hi = _bf(h)
    h_lo = _bf(h - h_hi.astype(jnp.float32))
    logits = _dot(h_hi, wrh_ref[...]) + (_dot(h_hi, wrl_ref[...]) + _dot(h_lo, wrh_ref[...]))
    lt = jnp.transpose(logits)[0:N_EXPERTS, :]
    row = lax.broadcasted_iota(jnp.int32, lt.shape, 0)
    neg = jnp.float32(-jnp.inf)
    m1 = jnp.max(lt, axis=0, keepdims=True)
    i1 = jnp.min(jnp.where(lt == m1, row, N_EXPERTS), axis=0, keepdims=True)
    rest = jnp.where(row == i1, neg, lt)
    m2 = jnp.max(rest, axis=0, keepdims=True)
    i2 = jnp.min(jnp.where(rest == m2, row, N_EXPERTS), axis=0, keepdims=True)
    e2 = jnp.exp(m2 - m1)
    den = 1.0 + e2
    g1 = 1.0 / den
    g2 = e2 / den
    sel1 = row == i1
    sel2 = row == i2
    oh = jnp.where(sel1 | sel2, 1.0, 0.0)

    sp = lax.broadcasted_iota(jnp.int32, (ts, ts), 0)
    sc = lax.broadcasted_iota(jnp.int32, (ts, ts), 1)
    upper = jnp.where(sp < sc, 1.0, 0.0).astype(jnp.bfloat16)
    rank = _dot(_bf(oh), upper)
    rk1 = jnp.sum(jnp.where(sel1, rank, 0.0), axis=0, keepdims=True)
    rk2 = jnp.sum(jnp.where(sel2, rank, 0.0), axis=0, keepdims=True)

    cnts, offs = [], []
    off = jnp.int32(0)
    for e in range(N_EXPERTS):
        c = jnp.sum(oh[e:e + 1, :]).astype(jnp.int32)
        cnts.append(c)
        offs.append(off)
        off = off + c
    off1 = jnp.zeros_like(rk1)
    off2 = jnp.zeros_like(rk2)
    for e in range(N_EXPERTS):
        fe = offs[e].astype(jnp.float32)
        off1 = jnp.where(i1 == e, fe, off1)
        off2 = jnp.where(i2 == e, fe, off2)
    pos1 = (off1 + rk1).astype(jnp.int32)
    pos2 = (off2 + rk2).astype(jnp.int32)
    pos_ref[0] = jnp.concatenate([pos1, pos2, jnp.zeros((SUBLANES - TOP_K, ts), jnp.int32)], axis=0)
    gate_ref[0] = jnp.concatenate([g1, g2, jnp.zeros((SUBLANES - TOP_K, ts), jnp.float32)], axis=0)
    for e in range(N_EXPERTS):
        cnt_ref[t * N_EXPERTS + e] = cnts[e]


def _route_call(x2, ln_w, wr_hi, wr_lo):
    n, d = x2.shape
    ts = min(ROW_TILE, n)
    nt = n // ts
    return pl.pallas_call(
        _route_kernel,
        out_shape=(
            jax.ShapeDtypeStruct((nt, SUBLANES, ts), jnp.int32),
            jax.ShapeDtypeStruct((nt, SUBLANES, ts), jnp.float32),
            jax.ShapeDtypeStruct((nt * N_EXPERTS,), jnp.int32),
        ),
        grid=(nt,),
        in_specs=[
            pl.BlockSpec((ts, d), lambda i: (i, 0)),
            pl.BlockSpec((1, d), lambda i: (0, 0)),
            pl.BlockSpec((d, LANES), lambda i: (0, 0)),
            pl.BlockSpec((d, LANES), lambda i: (0, 0)),
        ],
        out_specs=(
            pl.BlockSpec((1, SUBLANES, ts), lambda i: (i, 0, 0)),
            pl.BlockSpec((1, SUBLANES, ts), lambda i: (i, 0, 0)),
            pl.BlockSpec(memory_space=pltpu.SMEM),
        ),
        compiler_params=pltpu.CompilerParams(dimension_semantics=("arbitrary",), vmem_limit_bytes=VMEM_LIMIT),
        name="route",
    )(x2, ln_w, wr_hi, wr_lo)


def _dispatch_kernel(seg_ref, cnt_ref, zstart_ref, zlen_ref, nval_ref, x_ref, ln_ref, pos_ref, xs_hbm,
                     stage_ref, zero_ref, sem, zsem, *, max_tiles):
    ts = x_ref.shape[0]
    n_sorted = TOP_K * ts
    t = pl.program_id(0)
    nt = pl.num_programs(0)
    h = _bf(_rms(x_ref[...], ln_ref[...]))
    pos = pos_ref[0]
    perm = _one_hot_rows(pos[0:1, :], pos[1:2, :], n_sorted)
    sorted_h = _dot(perm, h)

    @pl.when(t > 0)
    def _():
        pltpu.make_async_copy(stage_ref, xs_hbm.at[pl.ds(0, n_sorted * REC_ROWS)], sem).wait()

    _store_records(stage_ref, sorted_h)

    off = jnp.int32(0)
    for e in range(N_EXPERTS):
        c = cnt_ref[t * N_EXPERTS + e]
        _copy_records(stage_ref, off, xs_hbm, seg_ref[t * N_EXPERTS + e], c, sem, ts)
        off = off + c

    @pl.when(t == nt - 1)
    def _():
        pltpu.make_async_copy(stage_ref, xs_hbm.at[pl.ds(0, n_sorted * REC_ROWS)], sem).wait()
        tm = zero_ref.shape[0] // REC_ROWS
        zero_ref[...] = jnp.zeros_like(zero_ref)

        def tail_fill(k):
            return pltpu.make_async_copy(
                zero_ref, xs_hbm.at[pl.ds(pl.multiple_of((nval_ref[0] + k) * tm * REC_ROWS, REC_ROWS),
                                          tm * REC_ROWS)], zsem)

        for wait in (False, True):
            for e in range(N_EXPERTS):
                _copy_records(zero_ref, None, xs_hbm, zstart_ref[e], zlen_ref[e], zsem, tm, wait=wait)
            for k in range(N_EXPERTS):
                @pl.when(nval_ref[0] + k < max_tiles)
                def _(k=k, wait=wait):
                    tail_fill(k).wait() if wait else tail_fill(k).start()


def _dispatch_call(seg, cnt, zstart, zlen, n_valid, x2, ln_w, pos, max_tiles):
    n, d = x2.shape
    ts = min(ROW_TILE, n)
    nt = n // ts
    return pl.pallas_call(
        functools.partial(_dispatch_kernel, max_tiles=max_tiles),
        out_shape=jax.ShapeDtypeStruct((max_tiles * ROW_TILE * REC_ROWS, LANES), jnp.float32),
        grid_spec=pltpu.PrefetchScalarGridSpec(
            num_scalar_prefetch=5,
            grid=(nt,),
            in_specs=[
                pl.BlockSpec((ts, d), lambda i, *_: (i, 0)),
                pl.BlockSpec((1, d), lambda i, *_: (0, 0)),
                pl.BlockSpec((1, SUBLANES, ts), lambda i, *_: (i, 0, 0)),
            ],
            out_specs=pl.BlockSpec(memory_space=pl.ANY),
            scratch_shapes=[
                pltpu.VMEM((TOP_K * ts * REC_ROWS, LANES), jnp.float32),
                pltpu.VMEM((ROW_TILE * REC_ROWS, LANES), jnp.float32),
                pltpu.SemaphoreType.DMA(()),
                pltpu.SemaphoreType.DMA(()),
            ],
        ),
        compiler_params=pltpu.CompilerParams(dimension_semantics=("arbitrary",), vmem_limit_bytes=VMEM_LIMIT),
        name="dispatch",
    )(seg, cnt, zstart, zlen, n_valid, x2, ln_w, pos)


def _ffn_group_kernel(texp_ref, tnext_ref, tswap_ref, nval_ref, xs_ref, wg_hbm, wu_hbm, wd_hbm, ys_ref,
                      wg_ref, wu_ref, wd_ref, sg_ref, su_ref, sd_ref, sems):
    j = pl.program_id(0)
    n_chunks = D_FF // FF_CHUNK
    ws = _WeightStream(wg_hbm, wu_hbm, wd_hbm, wg_ref, wu_ref, wd_ref, sg_ref, su_ref, sd_ref, sems)
    groups, start, install = ws.groups, ws.start, ws.install

    def load_h():
        tm = xs_ref.shape[0] // REC_ROWS
        return jnp.concatenate([_bf(w) for w in _load_records(xs_ref, tm)], axis=1)

    @pl.when(j == 0)
    def _():
        ws.load_all(texp_ref[0])

    valid = j < nval_ref[0]
    swap = tswap_ref[j] == 1
    not_ = jnp.logical_not

    @pl.when(not_(valid))
    def _():
        ys_ref[...] = jnp.zeros_like(ys_ref)

    @pl.when(valid & not_(swap))
    def _():
        _store_records(ys_ref, _swiglu(load_h(), wg_ref, wu_ref, wd_ref))

    @pl.when(valid & swap)
    def _():
        e = tnext_ref[j]
        start(e, 0)
        start(e, 1)
        h = load_h()
        y = None
        for f in range(n_chunks):
            part = _swiglu_chunk(h, wg_ref, wu_ref, wd_ref, f * FF_CHUNK)
            y = part if y is None else y + part
            g = f // SWAP_GROUP
            if (f + 1) * FF_CHUNK == groups[g][1]:
                install(e, g)
                if g + 2 < len(groups):
                    start(e, g + 2)
        _store_records(ys_ref, y)


def _ffn_group_call(tile_expert, tile_next, tile_swap, n_valid, xs, wg, wu, wd):
    rows = xs.shape[0]
    tm = ROW_TILE
    n_tiles = tile_expert.shape[0]
    d = wg.shape[1]
    return pl.pallas_call(
        _ffn_group_kernel,
        out_shape=jax.ShapeDtypeStruct((rows, LANES), jnp.float32),
        grid_spec=pltpu.PrefetchScalarGridSpec(
            num_scalar_prefetch=4,
            grid=(n_tiles,),
            in_specs=[
                pl.BlockSpec((tm * REC_ROWS, LANES), lambda j, *_: (j, 0)),
                pl.BlockSpec(memory_space=pl.ANY),
                pl.BlockSpec(memory_space=pl.ANY),
                pl.BlockSpec(memory_space=pl.ANY),
            ],
            out_specs=pl.BlockSpec((tm * REC_ROWS, LANES), lambda j, *_: (j, 0)),
            scratch_shapes=_weight_scratch(d),
        ),
        compiler_params=pltpu.CompilerParams(dimension_semantics=("arbitrary",), vmem_limit_bytes=VMEM_LIMIT),
        name="ffn_group",
    )(tile_expert, tile_next, tile_swap, n_valid, xs, wg, wu, wd)


def _combine_kernel(seg_ref, cnt_ref, x_ref, fin_ref, pos_ref, gate_ref, ys_hbm, o_ref, ybuf_ref, sems,
                    *, final_norm):
    ts = x_ref.shape[0]
    n_sorted = TOP_K * ts
    t = pl.program_id(0)
    nt = pl.num_programs(0)

    def fetch(tile, slot):
        off = jnp.int32(0)
        for e in range(N_EXPERTS):
            c = cnt_ref[tile * N_EXPERTS + e]
            _copy_records(ys_hbm, seg_ref[tile * N_EXPERTS + e], ybuf_ref.at[slot], off, c, sems.at[slot], ts)
            off = off + c

    @pl.when(t == 0)
    def _():
        fetch(0, 0)

    @pl.when(t + 1 < nt)
    def _():
        fetch(t + 1, (t + 1) % 2)

    slot = t % 2
    pltpu.make_async_copy(ys_hbm.at[pl.ds(0, n_sorted * REC_ROWS)], ybuf_ref.at[slot], sems.at[slot]).wait()
    ys = jnp.concatenate([_bf(w) for w in _load_records(ybuf_ref.at[slot], n_sorted)], axis=1)
    pos = pos_ref[0]
    gates = gate_ref[0]
    r = lax.broadcasted_iota(jnp.int32, (n_sorted, ts), 0)
    sel = _bf(jnp.where(r == pos[0:1, :], gates[0:1, :], 0.0) + jnp.where(r == pos[1:2, :], gates[1:2, :], 0.0))
    y = lax.dot_general(sel, ys, (((0,), (0,)), ((), ())), preferred_element_type=jnp.float32)
    out = x_ref[...] + y
    if final_norm:
        out = _rms(out, fin_ref[...])
    o_ref[...] = out


def _combine_call(seg, cnt, x2, fin_w, pos, gates, ys, *, final_norm):
    n, d = x2.shape
    ts = min(ROW_TILE, n)
    nt = n // ts
    return pl.pallas_call(
        functools.partial(_combine_kernel, final_norm=final_norm),
        out_shape=jax.ShapeDtypeStruct(x2.shape, x2.dtype),
        grid_spec=pltpu.PrefetchScalarGridSpec(
            num_scalar_prefetch=2,
            grid=(nt,),
            in_specs=[
                pl.BlockSpec((ts, d), lambda i, sg, ct: (i, 0)),
                pl.BlockSpec((1, d), lambda i, sg, ct: (0, 0)),
                pl.BlockSpec((1, SUBLANES, ts), lambda i, sg, ct: (i, 0, 0)),
                pl.BlockSpec((1, SUBLANES, ts), lambda i, sg, ct: (i, 0, 0)),
                pl.BlockSpec(memory_space=pl.ANY),
            ],
            out_specs=pl.BlockSpec((ts, d), lambda i, sg, ct: (i, 0)),
            scratch_shapes=[
                pltpu.VMEM((2, TOP_K * ts * REC_ROWS, LANES), jnp.float32),
                pltpu.SemaphoreType.DMA((2,)),
            ],
        ),
        compiler_params=pltpu.CompilerParams(dimension_semantics=("arbitrary",), vmem_limit_bytes=VMEM_LIMIT),
        name="combine",
    )(seg, cnt, x2, fin_w, pos, gates, ys)


def _moe(x2, ln_w, fin_w, w_router, wg, wu, wd, *, final_norm):
    n, d = x2.shape
    tm = ROW_TILE
    wr_p = jnp.concatenate([w_router, jnp.zeros((d, LANES - N_EXPERTS), w_router.dtype)], axis=1)
    wr_hi = _bf(wr_p)
    wr_lo = _bf(wr_p - wr_hi.astype(jnp.float32))
    pos, gates, cnt = _route_call(x2, ln_w, wr_hi, wr_lo)

    i32 = jnp.int32
    nt = cnt.shape[0] // N_EXPERTS
    cnt2 = cnt.reshape(nt, N_EXPERTS)
    totals = jnp.sum(cnt2, axis=0)
    tiles_e = (totals + tm - 1) // tm
    ends = jnp.cumsum(tiles_e)
    base = (ends - tiles_e) * tm
    seg = (base[None, :] + jnp.cumsum(cnt2, axis=0) - cnt2).reshape(-1).astype(i32)
    zstart = (base + totals).astype(i32)
    zlen = (tiles_e * tm - totals).astype(i32)
    n_valid = ends[-1].reshape(1).astype(i32)
    max_tiles = (TOP_K * n) // tm + N_EXPERTS
    j = jnp.minimum(jnp.arange(max_tiles, dtype=i32), n_valid - 1)
    tile_expert = jnp.sum((j[:, None] >= ends[None, :]).astype(i32), axis=1).astype(i32)
    tile_next = jnp.concatenate([tile_expert[1:], tile_expert[-1:]])
    tile_swap = (tile_next != tile_expert).astype(i32)

    xs = _dispatch_call(seg, cnt, zstart, zlen, n_valid, x2, ln_w, pos, max_tiles)
    ys = _ffn_group_call(tile_expert, tile_next, tile_swap, n_valid, xs, wg, wu, wd)
    return _combine_call(seg, cnt, x2, fin_w, pos, gates, ys, final_norm=final_norm)


def _prep_w_in(w):
    depth, d, _ = w.shape
    parts = [w[:, :, IN_COLS_A + GATE_RANK:], w[:, :, IN_COLS_A:IN_COLS_A + GATE_RANK],
             jnp.zeros((depth, d, GATE_PAD - GATE_RANK), w.dtype)]
    return _bf(w[:, :, :IN_COLS_A]), _bf(jnp.concatenate(parts, axis=2))


def kernel(x, ln1_w, w_in, w_gate_up, b_gate, gla_norm_w, conv_w, conv_b, cn_w, cn_b, w_out, ln2_w, wd_gate,
           wd_up, wd_down, w_router, we_gate, we_up, we_down, final_norm_w):
    bsz, seq, d = x.shape
    depth = ln1_w.shape[0]
    n = bsz * seq
    row = lambda a: a.reshape(1, -1)
    rows = lambda a: a[:, None, :]
    w_in_a, w_in_b = _prep_w_in(w_in)
    wgu_p = _bf(jnp.concatenate(
        [w_gate_up, jnp.zeros((depth, GATE_PAD - GATE_RANK, GLA_KEY), w_gate_up.dtype)], axis=1))
    cw_p = jnp.concatenate([conv_w, jnp.zeros((depth, CONV_HALO - CONV_WIDTH, CONV_CH), conv_w.dtype)], axis=1)
    w_out_b = _bf(w_out)
    for l in range(depth):
        x = _mixer_call(x, l, rows(ln1_w), w_in_a, w_in_b, wgu_p, rows(b_gate), rows(gla_norm_w), cw_p, rows(conv_b),
                        rows(cn_w), rows(cn_b), w_out_b)
        x2 = x.reshape(n, d)
        last = l == depth - 1
        i = l // 2
        if l % 2 == 0:
            x2 = _ffn_dense_call(x2, row(ln2_w[l]), row(final_norm_w), wd_gate[i:i + 1], wd_up[i:i + 1],
                                 wd_down[i:i + 1], final_norm=last)
        else:
            x2 = _moe(x2, row(ln2_w[l]), row(final_norm_w), w_router[i], we_gate[i], we_up[i], we_down[i],
                      final_norm=last)
        x = x2.reshape(bsz, seq, d)
    return x
```

```python
import functools

import jax
import jax.numpy as jnp
from jax import lax
from jax.experimental import pallas as pl
from jax.experimental.pallas import tpu as pltpu

D_MODEL = 1024
GLA_HEADS = 4
GLA_DK = 64
GLA_DV = 128
GLA_KEY = GLA_HEADS * GLA_DK
GLA_VAL = GLA_HEADS * GLA_DV
GATE_RANK = 16
GATE_NORMALIZER = 16.0
CONV_CH = 512
CONV_WIDTH = 31
D_FF = 2816
N_EXPERTS = 8
TOP_K = 2
EPS = 1e-6

LANES = 128
CHUNK = 64
TIME_TILE = 512
ROW_TILE = 512
FF_CHUNK = 256
SWAP_GROUP = 3
CONV_HALO = 32
GATE_PAD = LANES
IN_COLS_A = 2 * GLA_KEY + 2 * GLA_VAL
IN_COLS_B = GATE_PAD + 2 * CONV_CH
VMEM_LIMIT = 56 * 1024 * 1024
SUBLANES = 8
REC_ROWS = D_MODEL // LANES

_OQ, _OK, _OV, _OG = 0, GLA_KEY, 2 * GLA_KEY, 2 * GLA_KEY + GLA_VAL
_OGR, _OCA, _OCB = 0, GATE_PAD, GATE_PAD + CONV_CH

_LEVELS = (32, 16, 8, 4, 2, 1)


def _rms(x, w):
    return x * lax.rsqrt(jnp.mean(x * x, axis=-1, keepdims=True) + EPS) * w


def _bf(x):
    return x.astype(jnp.bfloat16)


def _dot(a, b):
    return jnp.dot(a, b, preferred_element_type=jnp.float32)


def _boundary(b, hs, rows):
    n, c = b.shape
    blk = 2 * hs
    if blk >= 8:
        b3 = b.reshape(n // blk, blk, c)
        return jnp.broadcast_to(b3[:, hs - 1:hs, :], (n // blk, blk, c)).reshape(n, c)
    y = pltpu.roll(b, n - (hs - 1), 0) if hs > 1 else b
    s = 1
    while s < blk:
        y = jnp.where((rows & s) != 0, pltpu.roll(y, s, 0), y)
        s *= 2
    return y


def _mixer_kernel(x_ref, ln_ref, wina_ref, winb_ref, wgu_ref, bg_ref, gnw_ref, cw_ref, cb_ref, cnw_ref, cnb_ref,
                  wout_ref, o_ref, s_ref, ubuf_ref, ush_ref, q_ref, y_ref):
    tt = x_ref.shape[1]
    nch = tt // CHUNK
    t = pl.program_id(1)

    @pl.when(t == 0)
    def _():
        s_ref[...] = jnp.zeros_like(s_ref)
        ubuf_ref[0:CONV_HALO, :] = jnp.zeros((CONV_HALO, CONV_CH), jnp.float32)
        ubuf_ref[CONV_HALO + tt:, :] = jnp.zeros((SUBLANES, CONV_CH), jnp.float32)

    x = x_ref[0]
    h = _bf(_rms(x, ln_ref[...]))
    zg = _dot(h, winb_ref[:, 0:GATE_PAD])
    za = _dot(h, wina_ref[...])
    zb = jnp.concatenate([zg, _dot(h, winb_ref[:, GATE_PAD:])], axis=1)

    q = za[:, _OQ:_OQ + GLA_KEY] * (GLA_DK ** -0.5)
    k = za[:, _OK:_OK + GLA_KEY]
    v = za[:, _OV:_OV + GLA_VAL]
    g = za[:, _OG:_OG + GLA_VAL]
    ca = zb[:, _OCA:_OCA + CONV_CH]
    cb = zb[:, _OCB:_OCB + CONV_CH]
    gr = zb[:, _OGR:_OGR + GATE_PAD]

    ubuf_ref[CONV_HALO:CONV_HALO + tt, :] = ca * jax.nn.sigmoid(cb)
    off0 = CONV_HALO - (CONV_WIDTH - 1)
    half = SUBLANES // 2
    ufull = ubuf_ref[...]
    n_u = ufull.shape[0]
    for r in range(1, half):
        ush_ref[r - 1] = pltpu.roll(ufull, n_u - r, 0)[0:n_u - SUBLANES, :]

    def taps(rows0, n_rows, c0, acc, upper):
        for j in range(CONV_WIDTH):
            r = (off0 + j) % SUBLANES
            if (r >= half) != upper:
                continue
            a0 = rows0 + off0 + j - r
            src = ubuf_ref if r % half == 0 else ush_ref.at[r % half - 1]
            acc = acc + cw_ref[j:j + 1, c0:c0 + LANES] * src[a0:a0 + n_rows, c0:c0 + LANES]
        return acc

    for c0 in range(0, CONV_CH, LANES):
        for r0 in list(range(0, tt, CHUNK)) + [tt]:
            n_rows = CHUNK if r0 < tt else SUBLANES
            q_ref[r0:r0 + n_rows, c0:c0 + LANES] = taps(r0, n_rows, c0, jnp.zeros((n_rows, LANES), jnp.float32),
                                                        True)
    for r0 in range(0, tt, CHUNK):
        for c0 in range(0, CONV_CH, LANES):
            acc = jnp.broadcast_to(cb_ref[:, c0:c0 + LANES], (CHUNK, LANES))
            acc = taps(r0, CHUNK, c0, acc, False)
            y_ref[r0:r0 + CHUNK, c0:c0 + LANES] = acc + q_ref[r0 + half:r0 + half + CHUNK, c0:c0 + LANES]
    ubuf_ref[0:CONV_HALO, :] = ubuf_ref[tt:tt + CONV_HALO, :]
    yc = y_ref[...]
    mu = jnp.mean(yc, axis=-1, keepdims=True)
    yd = yc - mu
    var = jnp.mean(yd * yd, axis=-1, keepdims=True)
    u = yd * lax.rsqrt(var + EPS) * cnw_ref[...] + cnb_ref[...]
    u = u * jax.nn.sigmoid(u)

    logit = _dot(_bf(gr), wgu_ref[...]) + bg_ref[...]
    la = jax.nn.log_sigmoid(logit) * (1.0 / GATE_NORMALIZER)
    rows = lax.broadcasted_iota(jnp.int32, (tt, GLA_KEY), 0)
    rc = rows & (CHUNK - 1)
    b = la
    s = 1
    while s < CHUNK:
        b = b + jnp.where(rc >= s, pltpu.roll(b, s, 0), 0.0)
        s *= 2
    b3 = b.reshape(nch, CHUNK, GLA_KEY)
    blast3 = b3[:, CHUNK - 1:CHUNK, :]
    blast = jnp.broadcast_to(blast3, (nch, CHUNK, GLA_KEY)).reshape(tt, GLA_KEY)
    qe = _bf(q * jnp.exp(b))
    kl = _bf(k * jnp.exp(blast - b))
    vb = _bf(v)
    dl = jnp.exp(blast3.reshape(nch, GLA_KEY))
    dl_t = jnp.transpose(jnp.concatenate([dl, jnp.zeros((LANES - nch, GLA_KEY), jnp.float32)], axis=0))

    qh, kh = [_bf(q)], [_bf(k)]
    for hs in _LEVELS:
        e = jnp.exp(-jnp.abs(b - _boundary(b, hs, rows)))
        qh.append(_bf(q * e))
        kh.append(_bf(k * e))

    ii = lax.broadcasted_iota(jnp.int32, (CHUNK, GLA_HEADS * CHUNK), 0)
    jj = lax.broadcasted_iota(jnp.int32, (CHUNK, GLA_HEADS * CHUNK), 1) & (CHUNK - 1)
    masks = [ii == jj]
    for hs in _LEVELS:
        blk = 2 * hs
        masks.append(((ii // blk) == (jj // blk)) & ((ii & (blk - 1)) >= hs) & ((jj & (blk - 1)) < hs))
    rk = lax.broadcasted_iota(jnp.int32, (GLA_HEADS * CHUNK, GLA_KEY), 0) // CHUNK
    ck = lax.broadcasted_iota(jnp.int32, (GLA_HEADS * CHUNK, GLA_KEY), 1) // GLA_DK
    bd_k = (rk == ck).astype(jnp.bfloat16)
    rv = lax.broadcasted_iota(jnp.int32, (GLA_HEADS * CHUNK, GLA_VAL), 0) // CHUNK
    cv = lax.broadcasted_iota(jnp.int32, (GLA_HEADS * CHUNK, GLA_VAL), 1) // GLA_DV
    bd_v = rv == cv
    bd_vb = bd_v.astype(jnp.bfloat16)

    st = s_ref[...]
    o_parts = []
    for c in range(nch):
        sl = slice(c * CHUNK, (c + 1) * CHUNK)
        att = jnp.zeros((CHUNK, GLA_HEADS * CHUNK), jnp.float32)
        for lvl in range(len(masks)):
            kbd = jnp.concatenate([kh[lvl][sl]] * GLA_HEADS, axis=0) * bd_k
            sc = lax.dot_general(qh[lvl][sl], kbd, (((1,), (1,)), ((), ())),
                                 preferred_element_type=jnp.float32)
            att = jnp.where(masks[lvl], sc, att)
        vbd = jnp.concatenate([vb[sl]] * GLA_HEADS, axis=0) * bd_vb
        o_parts.append(_dot(_bf(att), vbd) + _dot(qe[sl], _bf(st)))
        upd = lax.dot_general(kl[sl], vb[sl], (((0,), (0,)), ((), ())), preferred_element_type=jnp.float32)
        st = st * dl_t[:, c:c + 1] + jnp.where(bd_v, upd, 0.0)
    s_ref[...] = st
    o = jnp.concatenate(o_parts, axis=0)

    gnw = gnw_ref[...]
    heads = []
    for hd in range(GLA_HEADS):
        oh = o[:, hd * GLA_DV:(hd + 1) * GLA_DV]
        heads.append(_rms(oh, gnw))
    o = jnp.concatenate(heads, axis=-1) * (g * jax.nn.sigmoid(g))

    mix = _bf(jnp.concatenate([o, u], axis=-1))
    o_ref[0] = x + _dot(mix, wout_ref[...])


def _mixer_call(x, layer, ln_w, w_in_a, w_in_b, wgu_p, b_gate, gnw, conv_w, conv_b, cn_w, cn_b, w_out_b):
    bsz, seq, d = x.shape
    tt = min(TIME_TILE, seq)
    assert seq % tt == 0 and tt % CHUNK == 0
    const = lambda *shape: pl.BlockSpec((None,) + shape, lambda b, t: (layer,) + (0,) * len(shape))
    return pl.pallas_call(
        _mixer_kernel,
        out_shape=jax.ShapeDtypeStruct(x.shape, x.dtype),
        grid=(bsz, seq // tt),
        in_specs=[
            pl.BlockSpec((1, tt, d), lambda b, t: (b, t, 0)),
            const(1, d), const(d, IN_COLS_A), const(d, IN_COLS_B), const(GATE_PAD, GLA_KEY), const(1, GLA_KEY),
            const(1, GLA_DV),
            const(CONV_HALO, CONV_CH), const(1, CONV_CH), const(1, CONV_CH), const(1, CONV_CH),
            const(GLA_VAL + CONV_CH, d),
        ],
        out_specs=pl.BlockSpec((1, tt, d), lambda b, t: (b, t, 0)),
        scratch_shapes=[
            pltpu.VMEM((GLA_HEADS * GLA_DK, GLA_VAL), jnp.float32),
            pltpu.VMEM((CONV_HALO + tt + SUBLANES, CONV_CH), jnp.float32),
            pltpu.VMEM((SUBLANES // 2 - 1, CONV_HALO + tt, CONV_CH), jnp.float32),
            pltpu.VMEM((tt + SUBLANES, CONV_CH), jnp.float32),
            pltpu.VMEM((tt, CONV_CH), jnp.float32),
        ],
        compiler_params=pltpu.CompilerParams(dimension_semantics=("arbitrary", "arbitrary"),
                                             vmem_limit_bytes=VMEM_LIMIT),
        name="mixer",
    )(x, ln_w, w_in_a, w_in_b, wgu_p, b_gate, gnw, conv_w, conv_b, cn_w, cn_b, w_out_b)


def _swiglu_chunk(h, wg_ref, wu_ref, wd_ref, f0):
    gt = _dot(h, wg_ref[:, f0:f0 + FF_CHUNK])
    up = _dot(h, wu_ref[:, f0:f0 + FF_CHUNK])
    a = _bf(gt * jax.nn.sigmoid(gt) * up)
    return _dot(a, wd_ref[f0:f0 + FF_CHUNK, :])


def _swiglu(h, wg_ref, wu_ref, wd_ref):
    y = None
    for f0 in range(0, D_FF, FF_CHUNK):
        part = _swiglu_chunk(h, wg_ref, wu_ref, wd_ref, f0)
        y = part if y is None else y + part
    return y


class _WeightStream:
    def __init__(self, wg_hbm, wu_hbm, wd_hbm, wg_ref, wu_ref, wd_ref, sg_ref, su_ref, sd_ref, sems):
        self.hbm = (wg_hbm, wu_hbm, wd_hbm)
        self.dst = (wg_ref, wu_ref, wd_ref)
        self.stage = (sg_ref, su_ref, sd_ref)
        self.sems = sems
        gw = SWAP_GROUP * FF_CHUNK
        self.groups = [(c0, min(c0 + gw, D_FF)) for c0 in range(0, D_FF, gw)]

    def _copies(self, e, g):
        c0, c1 = self.groups[g]
        w, slot = c1 - c0, g % 2
        (wg_hbm, wu_hbm, wd_hbm), (sg_ref, su_ref, sd_ref) = self.hbm, self.stage
        sem = self.sems.at[slot]
        return (pltpu.make_async_copy(wg_hbm.at[e, :, pl.ds(c0, w)], sg_ref.at[slot, :, pl.ds(0, w)], sem),
                pltpu.make_async_copy(wu_hbm.at[e, :, pl.ds(c0, w)], su_ref.at[slot, :, pl.ds(0, w)], sem),
                pltpu.make_async_copy(wd_hbm.at[e, pl.ds(c0, w), :], sd_ref.at[slot, pl.ds(0, w), :], sem))

    def start(self, e, g):
        for cp in self._copies(e, g):
            cp.start()

    def install(self, e, g):
        for cp in self._copies(e, g):
            cp.wait()
        c0, c1 = self.groups[g]
        w, slot = c1 - c0, g % 2
        (wg_ref, wu_ref, wd_ref), (sg_ref, su_ref, sd_ref) = self.dst, self.stage
        wg_ref[:, c0:c1] = _bf(sg_ref[slot, :, 0:w])
        wu_ref[:, c0:c1] = _bf(su_ref[slot, :, 0:w])
        wd_ref[c0:c1, :] = _bf(sd_ref[slot, 0:w, :])

    def load_all(self, e):
        self.start(e, 0)
        for g in range(len(self.groups)):
            if g + 1 < len(self.groups):
                self.start(e, g + 1)
            self.install(e, g)


def _weight_scratch(d):
    gw = SWAP_GROUP * FF_CHUNK
    return [pltpu.VMEM((d, D_FF), jnp.bfloat16), pltpu.VMEM((d, D_FF), jnp.bfloat16),
            pltpu.VMEM((D_FF, d), jnp.bfloat16),
            pltpu.VMEM((2, d, gw), jnp.float32), pltpu.VMEM((2, d, gw), jnp.float32),
            pltpu.VMEM((2, gw, d), jnp.float32), pltpu.SemaphoreType.DMA((2,))]


def _ffn_dense_kernel(x_ref, ln_ref, fin_ref, wg_hbm, wu_hbm, wd_hbm, o_ref, wg_ref, wu_ref, wd_ref, sg_ref, su_ref,
                      sd_ref, sems, *, final_norm):
    @pl.when(pl.program_id(0) == 0)
    def _():
        _WeightStream(wg_hbm, wu_hbm, wd_hbm, wg_ref, wu_ref, wd_ref, sg_ref, su_ref, sd_ref, sems).load_all(0)

    x = x_ref[...]
    out = x + _swiglu(_bf(_rms(x, ln_ref[...])), wg_ref, wu_ref, wd_ref)
    if final_norm:
        out = _rms(out, fin_ref[...])
    o_ref[...] = out


def _ffn_dense_call(x2, ln_w, fin_w, wg, wu, wd, *, final_norm):
    n, d = x2.shape
    tm = min(ROW_TILE, n)
    assert n % tm == 0
    return pl.pallas_call(
        functools.partial(_ffn_dense_kernel, final_norm=final_norm),
        out_shape=jax.ShapeDtypeStruct(x2.shape, x2.dtype),
        grid=(n // tm,),
        in_specs=[
            pl.BlockSpec((tm, d), lambda i: (i, 0)),
            pl.BlockSpec((1, d), lambda i: (0, 0)),
            pl.BlockSpec((1, d), lambda i: (0, 0)),
            pl.BlockSpec(memory_space=pl.ANY),
            pl.BlockSpec(memory_space=pl.ANY),
            pl.BlockSpec(memory_space=pl.ANY),
        ],
        out_specs=pl.BlockSpec((tm, d), lambda i: (i, 0)),
        scratch_shapes=_weight_scratch(d),
        compiler_params=pltpu.CompilerParams(dimension_semantics=("arbitrary",), vmem_limit_bytes=VMEM_LIMIT),
        name="ffn_dense",
    )(x2, ln_w, fin_w, wg, wu, wd)


def _store_records(rec_ref, vals):
    m = vals.shape[0]
    for s in range(REC_ROWS):
        rec_ref[pl.ds(s, m, stride=REC_ROWS), :] = vals[:, s * LANES:(s + 1) * LANES]


def _load_records(rec_ref, m):
    return [rec_ref[pl.ds(s, m, stride=REC_ROWS), :] for s in range(REC_ROWS)]


def _copy_records(src_ref, src_off, dst_ref, dst_off, n, sem, max_rows, wait=False):
    bit = max_rows.bit_length() - 1
    while bit >= 0:
        size = (1 << bit) * REC_ROWS
        done = lax.shift_left(lax.shift_right_logical(n, bit + 1), bit + 1)
        src0 = 0 if src_off is None else pl.multiple_of((src_off + done) * REC_ROWS, REC_ROWS)
        dst0 = pl.multiple_of((dst_off + done) * REC_ROWS, REC_ROWS)

        @pl.when((lax.shift_right_logical(n, bit) & 1) == 1)
        def _(size=size, src0=src0, dst0=dst0):
            cp = pltpu.make_async_copy(src_ref.at[pl.ds(src0, size)], dst_ref.at[pl.ds(dst0, size)], sem)
            cp.wait() if wait else cp.start()
        bit -= 1


def _one_hot_rows(pos0, pos1, n_rows):
    r = lax.broadcasted_iota(jnp.int32, (n_rows, pos0.shape[1]), 0)
    return jnp.where((r == pos0) | (r == pos1), 1.0, 0.0).astype(jnp.bfloat16)


def _route_kernel(x_ref, ln_ref, wrh_ref, wrl_ref, pos_ref, gate_ref, cnt_ref):
    ts = x_ref.shape[0]
    t = pl.program_id(0)
    h = _rms(x_ref[...], ln_ref[...])
    h_hi = _bf(h)
    h_lo = _bf(h - h_hi.astype(jnp.float32))
    logits = _dot(h_hi, wrh_ref[...]) + (_dot(h_hi, wrl_ref[...]) + _dot(h_lo, wrh_ref[...]))
    lt = jnp.transpose(logits)[0:N_EXPERTS, :]
    row = lax.broadcasted_iota(jnp.int32, lt.shape, 0)
    neg = jnp.float32(-jnp.inf)
    m1 = jnp.max(lt, axis=0, keepdims=True)
    i1 = jnp.min(jnp.where(lt == m1, row, N_EXPERTS), axis=0, keepdims=True)
    rest = jnp.where(row == i1, neg, lt)
    m2 = jnp.max(rest, axis=0, keepdims=True)
    i2 = jnp.min(jnp.where(rest == m2, row, N_EXPERTS), axis=0, keepdims=True)
    e2 = jnp.exp(m2 - m1)
    den = 1.0 + e2
    g1 = 1.0 / den
    g2 = e2 / den
    sel1 = row == i1
    sel2 = row == i2
    oh = jnp.where(sel1 | sel2, 1.0, 0.0)

    sp = lax.broadcasted_iota(jnp.int32, (ts, ts), 0)
    sc = lax.broadcasted_iota(jnp.int32, (ts, ts), 1)
    upper = jnp.where(sp < sc, 1.0, 0.0).astype(jnp.bfloat16)
    rank = _dot(_bf(oh), upper)
    rk1 = jnp.sum(jnp.where(sel1, rank, 0.0), axis=0, keepdims=True)
    rk2 = jnp.sum(jnp.where(sel2, rank, 0.0), axis=0, keepdims=True)

    cnts, offs = [], []
    off = jnp.int32(0)
    for e in range(N_EXPERTS):
        c = jnp.sum(oh[e:e + 1, :]).astype(jnp.int32)
        cnts.append(c)
        offs.append(off)
        off = off + c
    off1 = jnp.zeros_like(rk1)
    off2 = jnp.zeros_like(rk2)
    for e in range(N_EXPERTS):
        fe = offs[e].astype(jnp.float32)
        off1 = jnp.where(i1 == e, fe, off1)
        off2 = jnp.where(i2 == e, fe, off2)
    pos1 = (off1 + rk1).astype(jnp.int32)
    pos2 = (off2 + rk2).astype(jnp.int32)
    pos_ref[0] = jnp.concatenate([pos1, pos2, jnp.zeros((SUBLANES - TOP_K, ts), jnp.int32)], axis=0)
    gate_ref[0] = jnp.concatenate([g1, g2, jnp.zeros((SUBLANES - TOP_K, ts), jnp.float32)], axis=0)
    for e in range(N_EXPERTS):
        cnt_ref[t * N_EXPERTS + e] = cnts[e]


def _route_call(x2, ln_w, wr_hi, wr_lo):
    n, d = x2.shape
    ts = min(ROW_TILE, n)
    nt = n // ts
    return pl.pallas_call(
        _route_kernel,
        out_shape=(
            jax.ShapeDtypeStruct((nt, SUBLANES, ts), jnp.int32),
            jax.ShapeDtypeStruct((nt, SUBLANES, ts), jnp.float32),
            jax.ShapeDtypeStruct((nt * N_EXPERTS,), jnp.int32),
        ),
        grid=(nt,),
        in_specs=[
            pl.BlockSpec((ts, d), lambda i: (i, 0)),
            pl.BlockSpec((1, d), lambda i: (0, 0)),
            pl.BlockSpec((d, LANES), lambda i: (0, 0)),
            pl.BlockSpec((d, LANES), lambda i: (0, 0)),
        ],
        out_specs=(
            pl.BlockSpec((1, SUBLANES, ts), lambda i: (i, 0, 0)),
            pl.BlockSpec((1, SUBLANES, ts), lambda i: (i, 0, 0)),
            pl.BlockSpec(memory_space=pltpu.SMEM),
        ),
        compiler_params=pltpu.CompilerParams(dimension_semantics=("arbitrary",), vmem_limit_bytes=VMEM_LIMIT),
        name="route",
    )(x2, ln_w, wr_hi, wr_lo)


def _dispatch_kernel(seg_ref, cnt_ref, zstart_ref, zlen_ref, nval_ref, x_ref, ln_ref, pos_ref, xs_hbm,
                     stage_ref, zero_ref, sem, zsem, *, max_tiles):
    ts = x_ref.shape[0]
    n_sorted = TOP_K * ts
    t = pl.program_id(0)
    nt = pl.num_programs(0)
    h = _bf(_rms(x_ref[...], ln_ref[...]))
    pos = pos_ref[0]
    perm = _one_hot_rows(pos[0:1, :], pos[1:2, :], n_sorted)
    sorted_h = _dot(perm, h)

    @pl.when(t > 0)
    def _():
        pltpu.make_async_copy(stage_ref, xs_hbm.at[pl.ds(0, n_sorted * REC_ROWS)], sem).wait()

    _store_records(stage_ref, sorted_h)

    off = jnp.int32(0)
    for e in range(N_EXPERTS):
        c = cnt_ref[t * N_EXPERTS + e]
        _copy_records(stage_ref, off, xs_hbm, seg_ref[t * N_EXPERTS + e], c, sem, ts)
        off = off + c

    @pl.when(t == nt - 1)
    def _():
        pltpu.make_async_copy(stage_ref, xs_hbm.at[pl.ds(0, n_sorted * REC_ROWS)], sem).wait()
        tm = zero_ref.shape[0] // REC_ROWS
        zero_ref[...] = jnp.zeros_like(zero_ref)

        def tail_fill(k):
            return pltpu.make_async_copy(
                zero_ref, xs_hbm.at[pl.ds(pl.multiple_of((nval_ref[0] + k) * tm * REC_ROWS, REC_ROWS),
                                          tm * REC_ROWS)], zsem)

        for wait in (False, True):
            for e in range(N_EXPERTS):
                _copy_records(zero_ref, None, xs_hbm, zstart_ref[e], zlen_ref[e], zsem, tm, wait=wait)
            for k in range(N_EXPERTS):
                @pl.when(nval_ref[0] + k < max_tiles)
                def _(k=k, wait=wait):
                    tail_fill(k).wait() if wait else tail_fill(k).start()


def _dispatch_call(seg, cnt, zstart, zlen, n_valid, x2, ln_w, pos, max_tiles):
    n, d = x2.shape
    ts = min(ROW_TILE, n)
    nt = n // ts
    return pl.pallas_call(
        functools.partial(_dispatch_kernel, max_tiles=max_tiles),
        out_shape=jax.ShapeDtypeStruct((max_tiles * ROW_TILE * REC_ROWS, LANES), jnp.float32),
        grid_spec=pltpu.PrefetchScalarGridSpec(
            num_scalar_prefetch=5,
            grid=(nt,),
            in_specs=[
                pl.BlockSpec((ts, d), lambda i, *_: (i, 0)),
                pl.BlockSpec((1, d), lambda i, *_: (0, 0)),
                pl.BlockSpec((1, SUBLANES, ts), lambda i, *_: (i, 0, 0)),
            ],
            out_specs=pl.BlockSpec(memory_space=pl.ANY),
            scratch_shapes=[
                pltpu.VMEM((TOP_K * ts * REC_ROWS, LANES), jnp.float32),
                pltpu.VMEM((ROW_TILE * REC_ROWS, LANES), jnp.float32),
                pltpu.SemaphoreType.DMA(()),
                pltpu.SemaphoreType.DMA(()),
            ],
        ),
        compiler_params=pltpu.CompilerParams(dimension_semantics=("arbitrary",), vmem_limit_bytes=VMEM_LIMIT),
        name="dispatch",
    )(seg, cnt, zstart, zlen, n_valid, x2, ln_w, pos)


def _ffn_group_kernel(texp_ref, tnext_ref, tswap_ref, nval_ref, xs_ref, wg_hbm, wu_hbm, wd_hbm, ys_ref,
                      wg_ref, wu_ref, wd_ref, sg_ref, su_ref, sd_ref, sems):
    j = pl.program_id(0)
    n_chunks = D_FF // FF_CHUNK
    ws = _WeightStream(wg_hbm, wu_hbm, wd_hbm, wg_ref, wu_ref, wd_ref, sg_ref, su_ref, sd_ref, sems)
    groups, start, install = ws.groups, ws.start, ws.install

    def load_h():
        tm = xs_ref.shape[0] // REC_ROWS
        return jnp.concatenate([_bf(w) for w in _load_records(xs_ref, tm)], axis=1)

    @pl.when(j == 0)
    def _():
        ws.load_all(texp_ref[0])

    valid = j < nval_ref[0]
    swap = tswap_ref[j] == 1
    not_ = jnp.logical_not

    @pl.when(not_(valid))
    def _():
        ys_ref[...] = jnp.zeros_like(ys_ref)

    @pl.when(valid & not_(swap))
    def _():
        _store_records(ys_ref, _swiglu(load_h(), wg_ref, wu_ref, wd_ref))

    @pl.when(valid & swap)
    def _():
        e = tnext_ref[j]
        start(e, 0)
        start(e, 1)
        h = load_h()
        y = None
        for f in range(n_chunks):
            part = _swiglu_chunk(h, wg_ref, wu_ref, wd_ref, f * FF_CHUNK)
            y = part if y is None else y + part
            g = f // SWAP_GROUP
            if (f + 1) * FF_CHUNK == groups[g][1]:
                install(e, g)
                if g + 2 < len(groups):
                    start(e, g + 2)
        _store_records(ys_ref, y)


def _ffn_group_call(tile_expert, tile_next, tile_swap, n_valid, xs, wg, wu, wd):
    rows = xs.shape[0]
    tm = ROW_TILE
    n_tiles = tile_expert.shape[0]
    d = wg.shape[1]
    return pl.pallas_call(
        _ffn_group_kernel,
        out_shape=jax.ShapeDtypeStruct((rows, LANES), jnp.float32),
        grid_spec=pltpu.PrefetchScalarGridSpec(
            num_scalar_prefetch=4,
            grid=(n_tiles,),
            in_specs=[
                pl.BlockSpec((tm * REC_ROWS, LANES), lambda j, *_: (j, 0)),
                pl.BlockSpec(memory_space=pl.ANY),
                pl.BlockSpec(memory_space=pl.ANY),
                pl.BlockSpec(memory_space=pl.ANY),
            ],
            out_specs=pl.BlockSpec((tm * REC_ROWS, LANES), lambda j, *_: (j, 0)),
            scratch_shapes=_weight_scratch(d),
        ),
        compiler_params=pltpu.CompilerParams(dimension_semantics=("arbitrary",), vmem_limit_bytes=VMEM_LIMIT),
        name="ffn_group",
    )(tile_expert, tile_next, tile_swap, n_valid, xs, wg, wu, wd)


def _combine_kernel(seg_ref, cnt_ref, x_ref, fin_ref, pos_ref, gate_ref, ys_hbm, o_ref, ybuf_ref, sems,
                    *, final_norm):
    ts = x_ref.shape[0]
    n_sorted = TOP_K * ts
    t = pl.program_id(0)
    nt = pl.num_programs(0)

    def fetch(tile, slot):
        off = jnp.int32(0)
        for e in range(N_EXPERTS):
            c = cnt_ref[tile * N_EXPERTS + e]
            _copy_records(ys_hbm, seg_ref[tile * N_EXPERTS + e], ybuf_ref.at[slot], off, c, sems.at[slot], ts)
            off = off + c

    @pl.when(t == 0)
    def _():
        fetch(0, 0)

    @pl.when(t + 1 < nt)
    def _():
        fetch(t + 1, (t + 1) % 2)

    slot = t % 2
    pltpu.make_async_copy(ys_hbm.at[pl.ds(0, n_sorted * REC_ROWS)], ybuf_ref.at[slot], sems.at[slot]).wait()
    ys = jnp.concatenate([_bf(w) for w in _load_records(ybuf_ref.at[slot], n_sorted)], axis=1)
    pos = pos_ref[0]
    gates = gate_ref[0]
    r = lax.broadcasted_iota(jnp.int32, (n_sorted, ts), 0)
    sel = _bf(jnp.where(r == pos[0:1, :], gates[0:1, :], 0.0) + jnp.where(r == pos[1:2, :], gates[1:2, :], 0.0))
    y = lax.dot_general(sel, ys, (((0,), (0,)), ((), ())), preferred_element_type=jnp.float32)
    out = x_ref[...] + y
    if final_norm:
        out = _rms(out, fin_ref[...])
    o_ref[...] = out


def _combine_call(seg, cnt, x2, fin_w, pos, gates, ys, *, final_norm):
    n, d = x2.shape
    ts = min(ROW_TILE, n)
    nt = n // ts
    return pl.pallas_call(
        functools.partial(_combine_kernel, final_norm=final_norm),
        out_shape=jax.ShapeDtypeStruct(x2.shape, x2.dtype),
        grid_spec=pltpu.PrefetchScalarGridSpec(
            num_scalar_prefetch=2,
            grid=(nt,),
            in_specs=[
                pl.BlockSpec((ts, d), lambda i, sg, ct: (i, 0)),
                pl.BlockSpec((1, d), lambda i, sg, ct: (0, 0)),
                pl.BlockSpec((1, SUBLANES, ts), lambda i, sg, ct: (i, 0, 0)),
                pl.BlockSpec((1, SUBLANES, ts), lambda i, sg, ct: (i, 0, 0)),
                pl.BlockSpec(memory_space=pl.ANY),
            ],
            out_specs=pl.BlockSpec((ts, d), lambda i, sg, ct: (i, 0)),
            scratch_shapes=[
                pltpu.VMEM((2, TOP_K * ts * REC_ROWS, LANES), jnp.float32),
                pltpu.SemaphoreType.DMA((2,)),
            ],
        ),
        compiler_params=pltpu.CompilerParams(dimension_semantics=("arbitrary",), vmem_limit_bytes=VMEM_LIMIT),
        name="combine",
    )(seg, cnt, x2, fin_w, pos, gates, ys)


def _moe(x2, ln_w, fin_w, w_router, wg, wu, wd, *, final_norm):
    n, d = x2.shape
    tm = ROW_TILE
    wr_p = jnp.concatenate([w_router, jnp.zeros((d, LANES - N_EXPERTS), w_router.dtype)], axis=1)
    wr_hi = _bf(wr_p)
    wr_lo = _bf(wr_p - wr_hi.astype(jnp.float32))
    pos, gates, cnt = _route_call(x2, ln_w, wr_hi, wr_lo)

    i32 = jnp.int32
    nt = cnt.shape[0] // N_EXPERTS
    cnt2 = cnt.reshape(nt, N_EXPERTS)
    totals = jnp.sum(cnt2, axis=0)
    tiles_e = (totals + tm - 1) // tm
    ends = jnp.cumsum(tiles_e)
    base = (ends - tiles_e) * tm
    seg = (base[None, :] + jnp.cumsum(cnt2, axis=0) - cnt2).reshape(-1).astype(i32)
    zstart = (base + totals).astype(i32)
    zlen = (tiles_e * tm - totals).astype(i32)
    n_valid = ends[-1].reshape(1).astype(i32)
    max_tiles = (TOP_K * n) // tm + N_EXPERTS
    j = jnp.minimum(jnp.arange(max_tiles, dtype=i32), n_valid - 1)
    tile_expert = jnp.sum((j[:, None] >= ends[None, :]).astype(i32), axis=1).astype(i32)
    tile_next = jnp.concatenate([tile_expert[1:], tile_expert[-1:]])
    tile_swap = (tile_next != tile_expert).astype(i32)

    xs = _dispatch_call(seg, cnt, zstart, zlen, n_valid, x2, ln_w, pos, max_tiles)
    ys = _ffn_group_call(tile_expert, tile_next, tile_swap, n_valid, xs, wg, wu, wd)
    return _combine_call(seg, cnt, x2, fin_w, pos, gates, ys, final_norm=final_norm)


def _prep_w_in(w):
    depth, d, _ = w.shape
    parts = [w[:, :, IN_COLS_A:IN_COLS_A + GATE_RANK], jnp.zeros((depth, d, GATE_PAD - GATE_RANK), w.dtype),
             w[:, :, IN_COLS_A + GATE_RANK:]]
    return _bf(w[:, :, :IN_COLS_A]), _bf(jnp.concatenate(parts, axis=2))


def kernel(x, ln1_w, w_in, w_gate_up, b_gate, gla_norm_w, conv_w, conv_b, cn_w, cn_b, w_out, ln2_w, wd_gate,
           wd_up, wd_down, w_router, we_gate, we_up, we_down, final_norm_w):
    bsz, seq, d = x.shape
    depth = ln1_w.shape[0]
    n = bsz * seq
    row = lambda a: a.reshape(1, -1)
    rows = lambda a: a[:, None, :]
    w_in_a, w_in_b = _prep_w_in(w_in)
    wgu_p = _bf(jnp.concatenate(
        [w_gate_up, jnp.zeros((depth, GATE_PAD - GATE_RANK, GLA_KEY), w_gate_up.dtype)], axis=1))
    cw_p = jnp.concatenate([conv_w, jnp.zeros((depth, CONV_HALO - CONV_WIDTH, CONV_CH), conv_w.dtype)], axis=1)
    w_out_b = _bf(w_out)
    for l in range(depth):
        x = _mixer_call(x, l, rows(ln1_w), w_in_a, w_in_b, wgu_p, rows(b_gate), rows(gla_norm_w), cw_p, rows(conv_b),
                        rows(cn_w), rows(cn_b), w_out_b)
        x2 = x.reshape(n, d)
        last = l == depth - 1
        i = l // 2
        if l % 2 == 0:
            x2 = _ffn_dense_call(x2, row(ln2_w[l]), row(final_norm_w), wd_gate[i:i + 1], wd_up[i:i + 1],
                                 wd_down[i:i + 1], final_norm=last)
        else:
            x2 = _moe(x2, row(ln2_w[l]), row(final_norm_w), w_router[i], we_gate[i], we_up[i], we_down[i],
                      final_norm=last)
        x = x2.reshape(bsz, seq, d)
    return x
```

```python
import functools

import jax
import jax.numpy as jnp
from jax import lax
from jax.experimental import pallas as pl
from jax.experimental.pallas import tpu as pltpu

D_MODEL = 1024
GLA_HEADS = 4
GLA_DK = 64
GLA_DV = 128
GLA_KEY = GLA_HEADS * GLA_DK
GLA_VAL = GLA_HEADS * GLA_DV
GATE_RANK = 16
GATE_NORMALIZER = 16.0
CONV_CH = 512
CONV_WIDTH = 31
D_FF = 2816
N_EXPERTS = 8
TOP_K = 2
EPS = 1e-6

LANES = 128
CHUNK = 64
TIME_TILE = 512
ROW_TILE = 512
FF_CHUNK = 256
SWAP_GROUP = 3
CONV_HALO = 32
GATE_PAD = LANES
IN_COLS_A = 2 * GLA_KEY + 2 * GLA_VAL
IN_COLS_B = GATE_PAD + 2 * CONV_CH
VMEM_LIMIT = 56 * 1024 * 1024
SUBLANES = 8
REC_ROWS = D_MODEL // LANES

_OQ, _OK, _OV, _OG = 0, GLA_KEY, 2 * GLA_KEY, 2 * GLA_KEY + GLA_VAL
_OGR, _OCA, _OCB = 0, GATE_PAD, GATE_PAD + CONV_CH

_LEVELS = (32, 16, 8, 4, 2, 1)


def _rms(x, w):
    return x * lax.rsqrt(jnp.mean(x * x, axis=-1, keepdims=True) + EPS) * w


def _bf(x):
    return x.astype(jnp.bfloat16)


def _dot(a, b):
    return jnp.dot(a, b, preferred_element_type=jnp.float32)


def _boundary(b, hs, rows):
    n, c = b.shape
    blk = 2 * hs
    if blk >= 8:
        b3 = b.reshape(n // blk, blk, c)
        return jnp.broadcast_to(b3[:, hs - 1:hs, :], (n // blk, blk, c)).reshape(n, c)
    y = pltpu.roll(b, n - (hs - 1), 0) if hs > 1 else b
    s = 1
    while s < blk:
        y = jnp.where((rows & s) != 0, pltpu.roll(y, s, 0), y)
        s *= 2
    return y


def _mixer_kernel(x_ref, ln_ref, wina_ref, winb_ref, wgu_ref, bg_ref, gnw_ref, cw_ref, cb_ref, cnw_ref, cnb_ref,
                  wout_ref, o_ref, s_ref, ubuf_ref, ush_ref, q_ref, y_ref):
    tt = x_ref.shape[1]
    nch = tt // CHUNK
    t = pl.program_id(1)

    @pl.when(t == 0)
    def _():
        s_ref[...] = jnp.zeros_like(s_ref)
        ubuf_ref[0:CONV_HALO, :] = jnp.zeros((CONV_HALO, CONV_CH), jnp.float32)
        ubuf_ref[CONV_HALO + tt:, :] = jnp.zeros((SUBLANES, CONV_CH), jnp.float32)

    x = x_ref[0]
    h = _bf(_rms(x, ln_ref[...]))
    zg = _dot(h, winb_ref[:, 0:GATE_PAD])
    logit = _dot(_bf(zg), wgu_ref[...]) + bg_ref[...]
    la = jax.nn.log_sigmoid(logit) * (1.0 / GATE_NORMALIZER)
    sixteen = jnp.uint32(16)
    abits = pltpu.bitcast(la[0:SUBLANES, 0:LANES], jnp.uint32)
    azero = _bf(lax.shift_right_logical(lax.shift_right_logical(abits, sixteen), sixteen)[0:1, :].astype(jnp.float32))
    h2 = jnp.concatenate([h[:, c:c + LANES] + azero for c in range(0, h.shape[1], LANES)], axis=1)
    za = _dot(h2, wina_ref[...])
    zb = jnp.concatenate([zg, _dot(h2, winb_ref[:, GATE_PAD:])], axis=1)

    q = za[:, _OQ:_OQ + GLA_KEY] * (GLA_DK ** -0.5)
    k = za[:, _OK:_OK + GLA_KEY]
    v = za[:, _OV:_OV + GLA_VAL]
    g = za[:, _OG:_OG + GLA_VAL]
    ca = zb[:, _OCA:_OCA + CONV_CH]
    cb = zb[:, _OCB:_OCB + CONV_CH]
    gr = zb[:, _OGR:_OGR + GATE_PAD]

    ubuf_ref[CONV_HALO:CONV_HALO + tt, :] = ca * jax.nn.sigmoid(cb)
    off0 = CONV_HALO - (CONV_WIDTH - 1)
    half = SUBLANES // 2
    ufull = ubuf_ref[...]
    n_u = ufull.shape[0]
    for r in range(1, half):
        ush_ref[r - 1] = pltpu.roll(ufull, n_u - r, 0)[0:n_u - SUBLANES, :]

    def taps(rows0, n_rows, c0, acc, upper):
        for j in range(CONV_WIDTH):
            r = (off0 + j) % SUBLANES
            if (r >= half) != upper:
                continue
            a0 = rows0 + off0 + j - r
            src = ubuf_ref if r % half == 0 else ush_ref.at[r % half - 1]
            acc = acc + cw_ref[j:j + 1, c0:c0 + LANES] * src[a0:a0 + n_rows, c0:c0 + LANES]
        return acc

    for c0 in range(0, CONV_CH, LANES):
        for r0 in list(range(0, tt, CHUNK)) + [tt]:
            n_rows = CHUNK if r0 < tt else SUBLANES
            q_ref[r0:r0 + n_rows, c0:c0 + LANES] = taps(r0, n_rows, c0, jnp.zeros((n_rows, LANES), jnp.float32),
                                                        True)
    for r0 in range(0, tt, CHUNK):
        for c0 in range(0, CONV_CH, LANES):
            acc = jnp.broadcast_to(cb_ref[:, c0:c0 + LANES], (CHUNK, LANES))
            acc = taps(r0, CHUNK, c0, acc, False)
            y_ref[r0:r0 + CHUNK, c0:c0 + LANES] = acc + q_ref[r0 + half:r0 + half + CHUNK, c0:c0 + LANES]
    ubuf_ref[0:CONV_HALO, :] = ubuf_ref[tt:tt + CONV_HALO, :]
    yc = y_ref[...]
    mu = jnp.mean(yc, axis=-1, keepdims=True)
    yd = yc - mu
    var = jnp.mean(yd * yd, axis=-1, keepdims=True)
    u = yd * lax.rsqrt(var + EPS) * cnw_ref[...] + cnb_ref[...]
    u = u * jax.nn.sigmoid(u)

    rows = lax.broadcasted_iota(jnp.int32, (tt, GLA_KEY), 0)
    rc = rows & (CHUNK - 1)
    b = la
    s = 1
    while s < CHUNK:
        b = b + jnp.where(rc >= s, pltpu.roll(b, s, 0), 0.0)
        s *= 2
    b3 = b.reshape(nch, CHUNK, GLA_KEY)
    blast3 = b3[:, CHUNK - 1:CHUNK, :]
    blast = jnp.broadcast_to(blast3, (nch, CHUNK, GLA_KEY)).reshape(tt, GLA_KEY)
    qe = _bf(q * jnp.exp(b))
    kl = _bf(k * jnp.exp(blast - b))
    vb = _bf(v)
    dl = jnp.exp(blast3.reshape(nch, GLA_KEY))
    dl_t = jnp.transpose(jnp.concatenate([dl, jnp.zeros((LANES - nch, GLA_KEY), jnp.float32)], axis=0))

    qh, kh = [_bf(q)], [_bf(k)]
    for hs in _LEVELS:
        e = jnp.exp(-jnp.abs(b - _boundary(b, hs, rows)))
        qh.append(_bf(q * e))
        kh.append(_bf(k * e))

    ii = lax.broadcasted_iota(jnp.int32, (CHUNK, GLA_HEADS * CHUNK), 0)
    jj = lax.broadcasted_iota(jnp.int32, (CHUNK, GLA_HEADS * CHUNK), 1) & (CHUNK - 1)
    masks = [ii == jj]
    for hs in _LEVELS:
        blk = 2 * hs
        masks.append(((ii // blk) == (jj // blk)) & ((ii & (blk - 1)) >= hs) & ((jj & (blk - 1)) < hs))
    rk = lax.broadcasted_iota(jnp.int32, (GLA_HEADS * CHUNK, GLA_KEY), 0) // CHUNK
    ck = lax.broadcasted_iota(jnp.int32, (GLA_HEADS * CHUNK, GLA_KEY), 1) // GLA_DK
    bd_k = (rk == ck).astype(jnp.bfloat16)
    rv = lax.broadcasted_iota(jnp.int32, (GLA_HEADS * CHUNK, GLA_VAL), 0) // CHUNK
    cv = lax.broadcasted_iota(jnp.int32, (GLA_HEADS * CHUNK, GLA_VAL), 1) // GLA_DV
    bd_v = rv == cv
    bd_vb = bd_v.astype(jnp.bfloat16)

    st = s_ref[...]
    o_parts = []
    for c in range(nch):
        sl = slice(c * CHUNK, (c + 1) * CHUNK)
        att = jnp.zeros((CHUNK, GLA_HEADS * CHUNK), jnp.float32)
        for lvl in range(len(masks)):
            kbd = jnp.concatenate([kh[lvl][sl]] * GLA_HEADS, axis=0) * bd_k
            sc = lax.dot_general(qh[lvl][sl], kbd, (((1,), (1,)), ((), ())),
                                 preferred_element_type=jnp.float32)
            att = jnp.where(masks[lvl], sc, att)
        vbd = jnp.concatenate([vb[sl]] * GLA_HEADS, axis=0) * bd_vb
        o_parts.append(_dot(_bf(att), vbd) + _dot(qe[sl], _bf(st)))
        upd = lax.dot_general(kl[sl], vb[sl], (((0,), (0,)), ((), ())), preferred_element_type=jnp.float32)
        st = st * dl_t[:, c:c + 1] + jnp.where(bd_v, upd, 0.0)
    s_ref[...] = st
    o = jnp.concatenate(o_parts, axis=0)

    gnw = gnw_ref[...]
    heads = []
    for hd in range(GLA_HEADS):
        oh = o[:, hd * GLA_DV:(hd + 1) * GLA_DV]
        heads.append(_rms(oh, gnw))
    o = jnp.concatenate(heads, axis=-1) * (g * jax.nn.sigmoid(g))

    mix = _bf(jnp.concatenate([o, u], axis=-1))
    o_ref[0] = x + _dot(mix, wout_ref[...])


def _mixer_call(x, layer, ln_w, w_in_a, w_in_b, wgu_p, b_gate, gnw, conv_w, conv_b, cn_w, cn_b, w_out_b):
    bsz, seq, d = x.shape
    tt = min(TIME_TILE, seq)
    assert seq % tt == 0 and tt % CHUNK == 0
    const = lambda *shape: pl.BlockSpec((None,) + shape, lambda b, t: (layer,) + (0,) * len(shape))
    return pl.pallas_call(
        _mixer_kernel,
        out_shape=jax.ShapeDtypeStruct(x.shape, x.dtype),
        grid=(bsz, seq // tt),
        in_specs=[
            pl.BlockSpec((1, tt, d), lambda b, t: (b, t, 0)),
            const(1, d), const(d, IN_COLS_A), const(d, IN_COLS_B), const(GATE_PAD, GLA_KEY), const(1, GLA_KEY),
            const(1, GLA_DV),
            const(CONV_HALO, CONV_CH), const(1, CONV_CH), const(1, CONV_CH), const(1, CONV_CH),
            const(GLA_VAL + CONV_CH, d),
        ],
        out_specs=pl.BlockSpec((1, tt, d), lambda b, t: (b, t, 0)),
        scratch_shapes=[
            pltpu.VMEM((GLA_HEADS * GLA_DK, GLA_VAL), jnp.float32),
            pltpu.VMEM((CONV_HALO + tt + SUBLANES, CONV_CH), jnp.float32),
            pltpu.VMEM((SUBLANES // 2 - 1, CONV_HALO + tt, CONV_CH), jnp.float32),
            pltpu.VMEM((tt + SUBLANES, CONV_CH), jnp.float32),
            pltpu.VMEM((tt, CONV_CH), jnp.float32),
        ],
        compiler_params=pltpu.CompilerParams(dimension_semantics=("arbitrary", "arbitrary"),
                                             vmem_limit_bytes=VMEM_LIMIT),
        name="mixer",
    )(x, ln_w, w_in_a, w_in_b, wgu_p, b_gate, gnw, conv_w, conv_b, cn_w, cn_b, w_out_b)


def _swiglu_chunk(h, wg_ref, wu_ref, wd_ref, f0):
    gt = _dot(h, wg_ref[:, f0:f0 + FF_CHUNK])
    up = _dot(h, wu_ref[:, f0:f0 + FF_CHUNK])
    a = _bf(gt * jax.nn.sigmoid(gt) * up)
    return _dot(a, wd_ref[f0:f0 + FF_CHUNK, :])


def _swiglu(h, wg_ref, wu_ref, wd_ref):
    y = None
    for f0 in range(0, D_FF, FF_CHUNK):
        part = _swiglu_chunk(h, wg_ref, wu_ref, wd_ref, f0)
        y = part if y is None else y + part
    return y


class _WeightStream:
    def __init__(self, wg_hbm, wu_hbm, wd_hbm, wg_ref, wu_ref, wd_ref, sg_ref, su_ref, sd_ref, sems):
        self.hbm = (wg_hbm, wu_hbm, wd_hbm)
        self.dst = (wg_ref, wu_ref, wd_ref)
        self.stage = (sg_ref, su_ref, sd_ref)
        self.sems = sems
        gw = SWAP_GROUP * FF_CHUNK
        self.groups = [(c0, min(c0 + gw, D_FF)) for c0 in range(0, D_FF, gw)]

    def _copies(self, e, g):
        c0, c1 = self.groups[g]
        w, slot = c1 - c0, g % 2
        (wg_hbm, wu_hbm, wd_hbm), (sg_ref, su_ref, sd_ref) = self.hbm, self.stage
        sem = self.sems.at[slot]
        return (pltpu.make_async_copy(wg_hbm.at[e, :, pl.ds(c0, w)], sg_ref.at[slot, :, pl.ds(0, w)], sem),
                pltpu.make_async_copy(wu_hbm.at[e, :, pl.ds(c0, w)], su_ref.at[slot, :, pl.ds(0, w)], sem),
                pltpu.make_async_copy(wd_hbm.at[e, pl.ds(c0, w), :], sd_ref.at[slot, pl.ds(0, w), :], sem))

    def start(self, e, g):
        for cp in self._copies(e, g):
            cp.start()

    def install(self, e, g):
        for cp in self._copies(e, g):
            cp.wait()
        c0, c1 = self.groups[g]
        w, slot = c1 - c0, g % 2
        (wg_ref, wu_ref, wd_ref), (sg_ref, su_ref, sd_ref) = self.dst, self.stage
        wg_ref[:, c0:c1] = _bf(sg_ref[slot, :, 0:w])
        wu_ref[:, c0:c1] = _bf(su_ref[slot, :, 0:w])
        wd_ref[c0:c1, :] = _bf(sd_ref[slot, 0:w, :])

    def load_all(self, e):
        self.start(e, 0)
        for g in range(len(self.groups)):
            if g + 1 < len(self.groups):
                self.start(e, g + 1)
            self.install(e, g)


def _weight_scratch(d):
    gw = SWAP_GROUP * FF_CHUNK
    return [pltpu.VMEM((d, D_FF), jnp.bfloat16), pltpu.VMEM((d, D_FF), jnp.bfloat16),
            pltpu.VMEM((D_FF, d), jnp.bfloat16),
            pltpu.VMEM((2, d, gw), jnp.float32), pltpu.VMEM((2, d, gw), jnp.float32),
            pltpu.VMEM((2, gw, d), jnp.float32), pltpu.SemaphoreType.DMA((2,))]


def _ffn_dense_kernel(x_ref, ln_ref, fin_ref, wg_hbm, wu_hbm, wd_hbm, o_ref, wg_ref, wu_ref, wd_ref, sg_ref, su_ref,
                      sd_ref, sems, *, final_norm):
    @pl.when(pl.program_id(0) == 0)
    def _():
        _WeightStream(wg_hbm, wu_hbm, wd_hbm, wg_ref, wu_ref, wd_ref, sg_ref, su_ref, sd_ref, sems).load_all(0)

    x = x_ref[...]
    out = x + _swiglu(_bf(_rms(x, ln_ref[...])), wg_ref, wu_ref, wd_ref)
    if final_norm:
        out = _rms(out, fin_ref[...])
    o_ref[...] = out


def _ffn_dense_call(x2, ln_w, fin_w, wg, wu, wd, *, final_norm):
    n, d = x2.shape
    tm = min(ROW_TILE, n)
    assert n % tm == 0
    return pl.pallas_call(
        functools.partial(_ffn_dense_kernel, final_norm=final_norm),
        out_shape=jax.ShapeDtypeStruct(x2.shape, x2.dtype),
        grid=(n // tm,),
        in_specs=[
            pl.BlockSpec((tm, d), lambda i: (i, 0)),
            pl.BlockSpec((1, d), lambda i: (0, 0)),
            pl.BlockSpec((1, d), lambda i: (0, 0)),
            pl.BlockSpec(memory_space=pl.ANY),
            pl.BlockSpec(memory_space=pl.ANY),
            pl.BlockSpec(memory_space=pl.ANY),
        ],
        out_specs=pl.BlockSpec((tm, d), lambda i: (i, 0)),
        scratch_shapes=_weight_scratch(d),
        compiler_params=pltpu.CompilerParams(dimension_semantics=("arbitrary",), vmem_limit_bytes=VMEM_LIMIT),
        name="ffn_dense",
    )(x2, ln_w, fin_w, wg, wu, wd)


def _store_records(rec_ref, vals):
    m = vals.shape[0]
    for s in range(REC_ROWS):
        rec_ref[pl.ds(s, m, stride=REC_ROWS), :] = vals[:, s * LANES:(s + 1) * LANES]


def _load_records(rec_ref, m):
    return [rec_ref[pl.ds(s, m, stride=REC_ROWS), :] for s in range(REC_ROWS)]


def _copy_records(src_ref, src_off, dst_ref, dst_off, n, sem, max_rows, wait=False):
    bit = max_rows.bit_length() - 1
    while bit >= 0:
        size = (1 << bit) * REC_ROWS
        done = lax.shift_left(lax.shift_right_logical(n, bit + 1), bit + 1)
        src0 = 0 if src_off is None else pl.multiple_of((src_off + done) * REC_ROWS, REC_ROWS)
        dst0 = pl.multiple_of((dst_off + done) * REC_ROWS, REC_ROWS)

        @pl.when((lax.shift_right_logical(n, bit) & 1) == 1)
        def _(size=size, src0=src0, dst0=dst0):
            cp = pltpu.make_async_copy(src_ref.at[pl.ds(src0, size)], dst_ref.at[pl.ds(dst0, size)], sem)
            cp.wait() if wait else cp.start()
        bit -= 1


def _one_hot_rows(pos0, pos1, n_rows):
    r = lax.broadcasted_iota(jnp.int32, (n_rows, pos0.shape[1]), 0)
    return jnp.where((r == pos0) | (r == pos1), 1.0, 0.0).astype(jnp.bfloat16)


def _route_kernel(x_ref, ln_ref, wrh_ref, wrl_ref, pos_ref, gate_ref, cnt_ref):
    ts = x_ref.shape[0]
    t = pl.program_id(0)
    h = _rms(x_ref[...], ln_ref[...])
    h_hi = _bf(h)
    h_lo = _bf(h - h_hi.astype(jnp.float32))
    logits = _dot(h_hi, wrh_ref[...]) + (_dot(h_hi, wrl_ref[...]) + _dot(h_lo, wrh_ref[...]))
    lt = jnp.transpose(logits)[0:N_EXPERTS, :]
    row = lax.broadcasted_iota(jnp.int32, lt.shape, 0)
    neg = jnp.float32(-jnp.inf)
    m1 = jnp.max(lt, axis=0, keepdims=True)
    i1 = jnp.min(jnp.where(lt == m1, row, N_EXPERTS), axis=0, keepdims=True)
    rest = jnp.where(row == i1, neg, lt)
    m2 = jnp.max(rest, axis=0, keepdims=True)
    i2 = jnp.min(jnp.where(rest == m2, row, N_EXPERTS), axis=0, keepdims=True)
    e2 = jnp.exp(m2 - m1)
    den = 1.0 + e2
    g1 = 1.0 / den
    g2 = e2 / den
    sel1 = row == i1
    sel2 = row == i2
    oh = jnp.where(sel1 | sel2, 1.0, 0.0)

    sp = lax.broadcasted_iota(jnp.int32, (ts, ts), 0)
    sc = lax.broadcasted_iota(jnp.int32, (ts, ts), 1)
    upper = jnp.where(sp < sc, 1.0, 0.0).astype(jnp.bfloat16)
    rank = _dot(_bf(oh), upper)
    rk1 = jnp.sum(jnp.where(sel1, rank, 0.0), axis=0, keepdims=True)
    rk2 = jnp.sum(jnp.where(sel2, rank, 0.0), axis=0, keepdims=True)

    cnts, offs = [], []
    off = jnp.int32(0)
    for e in range(N_EXPERTS):
        c = jnp.sum(oh[e:e + 1, :]).astype(jnp.int32)
        cnts.append(c)
        offs.append(off)
        off = off + c
    off1 = jnp.zeros_like(rk1)
    off2 = jnp.zeros_like(rk2)
    for e in range(N_EXPERTS):
        fe = offs[e].astype(jnp.float32)
        off1 = jnp.where(i1 == e, fe, off1)
        off2 = jnp.where(i2 == e, fe, off2)
    pos1 = (off1 + rk1).astype(jnp.int32)
    pos2 = (off2 + rk2).astype(jnp.int32)
    pos_ref[0] = jnp.concatenate([pos1, pos2, jnp.zeros((SUBLANES - TOP_K, ts), jnp.int32)], axis=0)
    gate_ref[0] = jnp.concatenate([g1, g2, jnp.zeros((SUBLANES - TOP_K, ts), jnp.float32)], axis=0)
    for e in range(N_EXPERTS):
        cnt_ref[t * N_EXPERTS + e] = cnts[e]


def _route_call(x2, ln_w, wr_hi, wr_lo):
    n, d = x2.shape
    ts = min(ROW_TILE, n)
    nt = n // ts
    return pl.pallas_call(
        _route_kernel,
        out_shape=(
            jax.ShapeDtypeStruct((nt, SUBLANES, ts), jnp.int32),
            jax.ShapeDtypeStruct((nt, SUBLANES, ts), jnp.float32),
            jax.ShapeDtypeStruct((nt * N_EXPERTS,), jnp.int32),
        ),
        grid=(nt,),
        in_specs=[
            pl.BlockSpec((ts, d), lambda i: (i, 0)),
            pl.BlockSpec((1, d), lambda i: (0, 0)),
            pl.BlockSpec((d, LANES), lambda i: (0, 0)),
            pl.BlockSpec((d, LANES), lambda i: (0, 0)),
        ],
        out_specs=(
            pl.BlockSpec((1, SUBLANES, ts), lambda i: (i, 0, 0)),
            pl.BlockSpec((1, SUBLANES, ts), lambda i: (i, 0, 0)),
            pl.BlockSpec(memory_space=pltpu.SMEM),
        ),
        compiler_params=pltpu.CompilerParams(dimension_semantics=("arbitrary",), vmem_limit_bytes=VMEM_LIMIT),
        name="route",
    )(x2, ln_w, wr_hi, wr_lo)


def _dispatch_kernel(seg_ref, cnt_ref, zstart_ref, zlen_ref, nval_ref, x_ref, ln_ref, pos_ref, xs_hbm,
                     stage_ref, zero_ref, sem, zsem, *, max_tiles):
    ts = x_ref.shape[0]
    n_sorted = TOP_K * ts
    t = pl.program_id(0)
    nt = pl.num_programs(0)
    h = _bf(_rms(x_ref[...], ln_ref[...]))
    pos = pos_ref[0]
    perm = _one_hot_rows(pos[0:1, :], pos[1:2, :], n_sorted)
    sorted_h = _dot(perm, h)

    @pl.when(t > 0)
    def _():
        pltpu.make_async_copy(stage_ref, xs_hbm.at[pl.ds(0, n_sorted * REC_ROWS)], sem).wait()

    _store_records(stage_ref, sorted_h)

    off = jnp.int32(0)
    for e in range(N_EXPERTS):
        c = cnt_ref[t * N_EXPERTS + e]
        _copy_records(stage_ref, off, xs_hbm, seg_ref[t * N_EXPERTS + e], c, sem, ts)
        off = off + c

    @pl.when(t == nt - 1)
    def _():
        pltpu.make_async_copy(stage_ref, xs_hbm.at[pl.ds(0, n_sorted * REC_ROWS)], sem).wait()
        tm = zero_ref.shape[0] // REC_ROWS
        zero_ref[...] = jnp.zeros_like(zero_ref)

        def tail_fill(k):
            return pltpu.make_async_copy(
                zero_ref, xs_hbm.at[pl.ds(pl.multiple_of((nval_ref[0] + k) * tm * REC_ROWS, REC_ROWS),
                                          tm * REC_ROWS)], zsem)

        for wait in (False, True):
            for e in range(N_EXPERTS):
                _copy_records(zero_ref, None, xs_hbm, zstart_ref[e], zlen_ref[e], zsem, tm, wait=wait)
            for k in range(N_EXPERTS):
                @pl.when(nval_ref[0] + k < max_tiles)
                def _(k=k, wait=wait):
                    tail_fill(k).wait() if wait else tail_fill(k).start()


def _dispatch_call(seg, cnt, zstart, zlen, n_valid, x2, ln_w, pos, max_tiles):
    n, d = x2.shape
    ts = min(ROW_TILE, n)
    nt = n // ts
    return pl.pallas_call(
        functools.partial(_dispatch_kernel, max_tiles=max_tiles),
        out_shape=jax.ShapeDtypeStruct((max_tiles * ROW_TILE * REC_ROWS, LANES), jnp.float32),
        grid_spec=pltpu.PrefetchScalarGridSpec(
            num_scalar_prefetch=5,
            grid=(nt,),
            in_specs=[
                pl.BlockSpec((ts, d), lambda i, *_: (i, 0)),
                pl.BlockSpec((1, d), lambda i, *_: (0, 0)),
                pl.BlockSpec((1, SUBLANES, ts), lambda i, *_: (i, 0, 0)),
            ],
            out_specs=pl.BlockSpec(memory_space=pl.ANY),
            scratch_shapes=[
                pltpu.VMEM((TOP_K * ts * REC_ROWS, LANES), jnp.float32),
                pltpu.VMEM((ROW_TILE * REC_ROWS, LANES), jnp.float32),
                pltpu.SemaphoreType.DMA(()),
                pltpu.SemaphoreType.DMA(()),
            ],
        ),
        compiler_params=pltpu.CompilerParams(dimension_semantics=("arbitrary",), vmem_limit_bytes=VMEM_LIMIT),
        name="dispatch",
    )(seg, cnt, zstart, zlen, n_valid, x2, ln_w, pos)


def _ffn_group_kernel(texp_ref, tnext_ref, tswap_ref, nval_ref, xs_ref, wg_hbm, wu_hbm, wd_hbm, ys_ref,
                      wg_ref, wu_ref, wd_ref, sg_ref, su_ref, sd_ref, sems):
    j = pl.program_id(0)
    n_chunks = D_FF // FF_CHUNK
    ws = _WeightStream(wg_hbm, wu_hbm, wd_hbm, wg_ref, wu_ref, wd_ref, sg_ref, su_ref, sd_ref, sems)
    groups, start, install = ws.groups, ws.start, ws.install

    def load_h():
        tm = xs_ref.shape[0] // REC_ROWS
        return jnp.concatenate([_bf(w) for w in _load_records(xs_ref, tm)], axis=1)

    @pl.when(j == 0)
    def _():
        ws.load_all(texp_ref[0])

    valid = j < nval_ref[0]
    swap = tswap_ref[j] == 1
    not_ = jnp.logical_not

    @pl.when(not_(valid))
    def _():
        ys_ref[...] = jnp.zeros_like(ys_ref)

    @pl.when(valid & not_(swap))
    def _():
        _store_records(ys_ref, _swiglu(load_h(), wg_ref, wu_ref, wd_ref))

    @pl.when(valid & swap)
    def _():
        e = tnext_ref[j]
        start(e, 0)
        start(e, 1)
        h = load_h()
        y = None
        for f in range(n_chunks):
            part = _swiglu_chunk(h, wg_ref, wu_ref, wd_ref, f * FF_CHUNK)
            y = part if y is None else y + part
            g = f // SWAP_GROUP
            if (f + 1) * FF_CHUNK == groups[g][1]:
                install(e, g)
                if g + 2 < len(groups):
                    start(e, g + 2)
        _store_records(ys_ref, y)


def _ffn_group_call(tile_expert, tile_next, tile_swap, n_valid, xs, wg, wu, wd):
    rows = xs.shape[0]
    tm = ROW_TILE
    n_tiles = tile_expert.shape[0]
    d = wg.shape[1]
    return pl.pallas_call(
        _ffn_group_kernel,
        out_shape=jax.ShapeDtypeStruct((rows, LANES), jnp.float32),
        grid_spec=pltpu.PrefetchScalarGridSpec(
            num_scalar_prefetch=4,
            grid=(n_tiles,),
            in_specs=[
                pl.BlockSpec((tm * REC_ROWS, LANES), lambda j, *_: (j, 0)),
                pl.BlockSpec(memory_space=pl.ANY),
                pl.BlockSpec(memory_space=pl.ANY),
                pl.BlockSpec(memory_space=pl.ANY),
            ],
            out_specs=pl.BlockSpec((tm * REC_ROWS, LANES), lambda j, *_: (j, 0)),
            scratch_shapes=_weight_scratch(d),
        ),
        compiler_params=pltpu.CompilerParams(dimension_semantics=("arbitrary",), vmem_limit_bytes=VMEM_LIMIT),
        name="ffn_group",
    )(tile_expert, tile_next, tile_swap, n_valid, xs, wg, wu, wd)


def _combine_kernel(seg_ref, cnt_ref, x_ref, fin_ref, pos_ref, gate_ref, ys_hbm, o_ref, ybuf_ref, sems,
                    *, final_norm):
    ts = x_ref.shape[0]
    n_sorted = TOP_K * ts
    t = pl.program_id(0)
    nt = pl.num_programs(0)

    def fetch(tile, slot):
        off = jnp.int32(0)
        for e in range(N_EXPERTS):
            c = cnt_ref[tile * N_EXPERTS + e]
            _copy_records(ys_hbm, seg_ref[tile * N_EXPERTS + e], ybuf_ref.at[slot], off, c, sems.at[slot], ts)
            off = off + c

    @pl.when(t == 0)
    def _():
        fetch(0, 0)

    @pl.when(t + 1 < nt)
    def _():
        fetch(t + 1, (t + 1) % 2)

    slot = t % 2
    pltpu.make_async_copy(ys_hbm.at[pl.ds(0, n_sorted * REC_ROWS)], ybuf_ref.at[slot], sems.at[slot]).wait()
    ys = jnp.concatenate([_bf(w) for w in _load_records(ybuf_ref.at[slot], n_sorted)], axis=1)
    pos = pos_ref[0]
    gates = gate_ref[0]
    r = lax.broadcasted_iota(jnp.int32, (n_sorted, ts), 0)
    sel = _bf(jnp.where(r == pos[0:1, :], gates[0:1, :], 0.0) + jnp.where(r == pos[1:2, :], gates[1:2, :], 0.0))
    y = lax.dot_general(sel, ys, (((0,), (0,)), ((), ())), preferred_element_type=jnp.float32)
    out = x_ref[...] + y
    if final_norm:
        out = _rms(out, fin_ref[...])
    o_ref[...] = out


def _combine_call(seg, cnt, x2, fin_w, pos, gates, ys, *, final_norm):
    n, d = x2.shape
    ts = min(ROW_TILE, n)
    nt = n // ts
    return pl.pallas_call(
        functools.partial(_combine_kernel, final_norm=final_norm),
        out_shape=jax.ShapeDtypeStruct(x2.shape, x2.dtype),
        grid_spec=pltpu.PrefetchScalarGridSpec(
            num_scalar_prefetch=2,
            grid=(nt,),
            in_specs=[
                pl.BlockSpec((ts, d), lambda i, sg, ct: (i, 0)),
                pl.BlockSpec((1, d), lambda i, sg, ct: (0, 0)),
                pl.BlockSpec((1, SUBLANES, ts), lambda i, sg, ct: (i, 0, 0)),
                pl.BlockSpec((1, SUBLANES, ts), lambda i, sg, ct: (i, 0, 0)),
                pl.BlockSpec(memory_space=pl.ANY),
            ],
            out_specs=pl.BlockSpec((ts, d), lambda i, sg, ct: (i, 0)),
            scratch_shapes=[
                pltpu.VMEM((2, TOP_K * ts * REC_ROWS, LANES), jnp.float32),
                pltpu.SemaphoreType.DMA((2,)),
            ],
        ),
        compiler_params=pltpu.CompilerParams(dimension_semantics=("arbitrary",), vmem_limit_bytes=VMEM_LIMIT),
        name="combine",
    )(seg, cnt, x2, fin_w, pos, gates, ys)


def _moe(x2, ln_w, fin_w, w_router, wg, wu, wd, *, final_norm):
    n, d = x2.shape
    tm = ROW_TILE
    wr_p = jnp.concatenate([w_router, jnp.zeros((d, LANES - N_EXPERTS), w_router.dtype)], axis=1)
    wr_hi = _bf(wr_p)
    wr_lo = _bf(wr_p - wr_hi.astype(jnp.float32))
    pos, gates, cnt = _route_call(x2, ln_w, wr_hi, wr_lo)

    i32 = jnp.int32
    nt = cnt.shape[0] // N_EXPERTS
    cnt2 = cnt.reshape(nt, N_EXPERTS)
    totals = jnp.sum(cnt2, axis=0)
    tiles_e = (totals + tm - 1) // tm
    ends = jnp.cumsum(tiles_e)
    base = (ends - tiles_e) * tm
    seg = (base[None, :] + jnp.cumsum(cnt2, axis=0) - cnt2).reshape(-1).astype(i32)
    zstart = (base + totals).astype(i32)
    zlen = (tiles_e * tm - totals).astype(i32)
    n_valid = ends[-1].reshape(1).astype(i32)
    max_tiles = (TOP_K * n) // tm + N_EXPERTS
    j = jnp.minimum(jnp.arange(max_tiles, dtype=i32), n_valid - 1)
    tile_expert = jnp.sum((j[:, None] >= ends[None, :]).astype(i32), axis=1).astype(i32)
    tile_next = jnp.concatenate([tile_expert[1:], tile_expert[-1:]])
    tile_swap = (tile_next != tile_expert).astype(i32)

    xs = _dispatch_call(seg, cnt, zstart, zlen, n_valid, x2, ln_w, pos, max_tiles)
    ys = _ffn_group_call(tile_expert, tile_next, tile_swap, n_valid, xs, wg, wu, wd)
    return _combine_call(seg, cnt, x2, fin_w, pos, gates, ys, final_norm=final_norm)


def _prep_w_in(w):
    depth, d, _ = w.shape
    parts = [w[:, :, IN_COLS_A:IN_COLS_A + GATE_RANK], jnp.zeros((depth, d, GATE_PAD - GATE_RANK), w.dtype),
             w[:, :, IN_COLS_A + GATE_RANK:]]
    return _bf(w[:, :, :IN_COLS_A]), _bf(jnp.concatenate(parts, axis=2))


def kernel(x, ln1_w, w_in, w_gate_up, b_gate, gla_norm_w, conv_w, conv_b, cn_w, cn_b, w_out, ln2_w, wd_gate,
           wd_up, wd_down, w_router, we_gate, we_up, we_down, final_norm_w):
    bsz, seq, d = x.shape
    depth = ln1_w.shape[0]
    n = bsz * seq
    row = lambda a: a.reshape(1, -1)
    rows = lambda a: a[:, None, :]
    w_in_a, w_in_b = _prep_w_in(w_in)
    wgu_p = _bf(jnp.concatenate(
        [w_gate_up, jnp.zeros((depth, GATE_PAD - GATE_RANK, GLA_KEY), w_gate_up.dtype)], axis=1))
    cw_p = jnp.concatenate([conv_w, jnp.zeros((depth, CONV_HALO - CONV_WIDTH, CONV_CH), conv_w.dtype)], axis=1)
    w_out_b = _bf(w_out)
    for l in range(depth):
        x = _mixer_call(x, l, rows(ln1_w), w_in_a, w_in_b, wgu_p, rows(b_gate), rows(gla_norm_w), cw_p, rows(conv_b),
                        rows(cn_w), rows(cn_b), w_out_b)
        x2 = x.reshape(n, d)
        last = l == depth - 1
        i = l // 2
        if l % 2 == 0:
            x2 = _ffn_dense_call(x2, row(ln2_w[l]), row(final_norm_w), wd_gate[i:i + 1], wd_up[i:i + 1],
                                 wd_down[i:i + 1], final_norm=last)
        else:
            x2 = _moe(x2, row(ln2_w[l]), row(final_norm_w), w_router[i], we_gate[i], we_up[i], we_down[i],
                      final_norm=last)
        x = x2.reshape(bsz, seq, d)
    return x
```

```python
import functools

import jax
import jax.numpy as jnp
from jax import lax
from jax.experimental import pallas as pl
from jax.experimental.pallas import tpu as pltpu

D_MODEL = 1024
GLA_HEADS = 4
GLA_DK = 64
GLA_DV = 128
GLA_KEY = GLA_HEADS * GLA_DK
GLA_VAL = GLA_HEADS * GLA_DV
GATE_RANK = 16
GATE_NORMALIZER = 16.0
CONV_CH = 512
CONV_WIDTH = 31
D_FF = 2816
N_EXPERTS = 8
TOP_K = 2
EPS = 1e-6

LANES = 128
CHUNK = 64
TIME_TILE = 512
ROW_TILE = 512
FF_CHUNK = 256
SWAP_GROUP = 3
CONV_HALO = 32
GATE_PAD = LANES
IN_COLS_A = 2 * GLA_KEY + 2 * GLA_VAL
IN_COLS_B = GATE_PAD + 2 * CONV_CH
VMEM_LIMIT = 56 * 1024 * 1024
SUBLANES = 8
REC_ROWS = D_MODEL // LANES

_OQ, _OK, _OV, _OG = 0, GLA_KEY, 2 * GLA_KEY, 2 * GLA_KEY + GLA_VAL
_OGR, _OCA, _OCB = 0, GATE_PAD, GATE_PAD + CONV_CH

_LEVELS = (32, 16, 8, 4, 2, 1)


def _rms(x, w):
    return x * lax.rsqrt(jnp.mean(x * x, axis=-1, keepdims=True) + EPS) * w


def _bf(x):
    return x.astype(jnp.bfloat16)


def _dot(a, b):
    return jnp.dot(a, b, preferred_element_type=jnp.float32)


def _boundary(b, hs, rows):
    n, c = b.shape
    blk = 2 * hs
    if blk >= 8:
        b3 = b.reshape(n // blk, blk, c)
        return jnp.broadcast_to(b3[:, hs - 1:hs, :], (n // blk, blk, c)).reshape(n, c)
    y = pltpu.roll(b, n - (hs - 1), 0) if hs > 1 else b
    s = 1
    while s < blk:
        y = jnp.where((rows & s) != 0, pltpu.roll(y, s, 0), y)
        s *= 2
    return y


def _mixer_kernel(x_ref, ln_ref, wina_ref, winb_ref, wgu_ref, bg_ref, gnw_ref, cw_ref, cb_ref, cnw_ref, cnb_ref,
                  wout_ref, o_ref, s_ref, ubuf_ref, ush_ref, q_ref, y_ref):
    tt = x_ref.shape[1]
    nch = tt // CHUNK
    t = pl.program_id(1)

    @pl.when(t == 0)
    def _():
        s_ref[...] = jnp.zeros_like(s_ref)
        ubuf_ref[0:CONV_HALO, :] = jnp.zeros((CONV_HALO, CONV_CH), jnp.float32)
        ubuf_ref[CONV_HALO + tt:, :] = jnp.zeros((SUBLANES, CONV_CH), jnp.float32)

    x = x_ref[0]
    h = _bf(_rms(x, ln_ref[...]))
    zg = _dot(h, winb_ref[:, 0:GATE_PAD])
    logit = _dot(_bf(zg), wgu_ref[...]) + bg_ref[...]
    la = jax.nn.log_sigmoid(logit) * (1.0 / GATE_NORMALIZER)
    sixteen = jnp.uint32(16)
    abits = pltpu.bitcast(la[0:SUBLANES, 0:LANES], jnp.uint32)
    azero = _bf(lax.shift_right_logical(lax.shift_right_logical(abits, sixteen), sixteen)[0:1, :].astype(jnp.float32))
    h2 = jnp.concatenate([h[:, c:c + LANES] + azero for c in range(0, h.shape[1], LANES)], axis=1)
    za = _dot(h2, wina_ref[...])
    zb = jnp.concatenate([zg, _dot(h2, winb_ref[:, GATE_PAD:])], axis=1)

    q = za[:, _OQ:_OQ + GLA_KEY] * (GLA_DK ** -0.5)
    k = za[:, _OK:_OK + GLA_KEY]
    v = za[:, _OV:_OV + GLA_VAL]
    g = za[:, _OG:_OG + GLA_VAL]
    ca = zb[:, _OCA:_OCA + CONV_CH]
    cb = zb[:, _OCB:_OCB + CONV_CH]
    gr = zb[:, _OGR:_OGR + GATE_PAD]

    ubuf_ref[CONV_HALO:CONV_HALO + tt, :] = ca * jax.nn.sigmoid(cb)
    off0 = CONV_HALO - (CONV_WIDTH - 1)
    half = SUBLANES // 2
    ufull = ubuf_ref[...]
    n_u = ufull.shape[0]
    for r in range(1, half):
        ush_ref[r - 1] = pltpu.roll(ufull, n_u - r, 0)[0:n_u - SUBLANES, :]

    def taps(rows0, n_rows, c0, acc, upper):
        for j in range(CONV_WIDTH):
            r = (off0 + j) % SUBLANES
            if (r >= half) != upper:
                continue
            a0 = rows0 + off0 + j - r
            src = ubuf_ref if r % half == 0 else ush_ref.at[r % half - 1]
            acc = acc + cw_ref[j:j + 1, c0:c0 + LANES] * src[a0:a0 + n_rows, c0:c0 + LANES]
        return acc

    for c0 in range(0, CONV_CH, LANES):
        for r0 in list(range(0, tt, CHUNK)) + [tt]:
            n_rows = CHUNK if r0 < tt else SUBLANES
            q_ref[r0:r0 + n_rows, c0:c0 + LANES] = taps(r0, n_rows, c0, jnp.zeros((n_rows, LANES), jnp.float32),
                                                        True)
    for r0 in range(0, tt, CHUNK):
        for c0 in range(0, CONV_CH, LANES):
            acc = jnp.broadcast_to(cb_ref[:, c0:c0 + LANES], (CHUNK, LANES))
            acc = taps(r0, CHUNK, c0, acc, False)
            y_ref[r0:r0 + CHUNK, c0:c0 + LANES] = acc + q_ref[r0 + half:r0 + half + CHUNK, c0:c0 + LANES]
    ubuf_ref[0:CONV_HALO, :] = ubuf_ref[tt:tt + CONV_HALO, :]
    yc = y_ref[...]
    mu = jnp.mean(yc, axis=-1, keepdims=True)
    yd = yc - mu
    var = jnp.mean(yd * yd, axis=-1, keepdims=True)
    u = yd * lax.rsqrt(var + EPS) * cnw_ref[...] + cnb_ref[...]
    u = u * jax.nn.sigmoid(u)

    rows = lax.broadcasted_iota(jnp.int32, (tt, GLA_KEY), 0)
    rc = rows & (CHUNK - 1)
    b = la
    s = 1
    while s < CHUNK:
        b = b + jnp.where(rc >= s, pltpu.roll(b, s, 0), 0.0)
        s *= 2
    b3 = b.reshape(nch, CHUNK, GLA_KEY)
    blast3 = b3[:, CHUNK - 1:CHUNK, :]
    blast = jnp.broadcast_to(blast3, (nch, CHUNK, GLA_KEY)).reshape(tt, GLA_KEY)
    qe = _bf(q * jnp.exp(b))
    kl = _bf(k * jnp.exp(blast - b))
    vb = _bf(v)
    dl = jnp.exp(blast3.reshape(nch, GLA_KEY))
    dl_t = jnp.transpose(jnp.concatenate([dl, jnp.zeros((LANES - nch, GLA_KEY), jnp.float32)], axis=0))

    qh, kh = [_bf(q)], [_bf(k)]
    for hs in _LEVELS:
        e = jnp.exp(-jnp.abs(b - _boundary(b, hs, rows)))
        qh.append(_bf(q * e))
        kh.append(_bf(k * e))

    ii = lax.broadcasted_iota(jnp.int32, (CHUNK, GLA_HEADS * CHUNK), 0)
    jj = lax.broadcasted_iota(jnp.int32, (CHUNK, GLA_HEADS * CHUNK), 1) & (CHUNK - 1)
    masks = [ii == jj]
    for hs in _LEVELS:
        blk = 2 * hs
        masks.append(((ii // blk) == (jj // blk)) & ((ii & (blk - 1)) >= hs) & ((jj & (blk - 1)) < hs))
    rk = lax.broadcasted_iota(jnp.int32, (GLA_HEADS * CHUNK, GLA_KEY), 0) // CHUNK
    ck = lax.broadcasted_iota(jnp.int32, (GLA_HEADS * CHUNK, GLA_KEY), 1) // GLA_DK
    bd_k = (rk == ck).astype(jnp.bfloat16)
    rv = lax.broadcasted_iota(jnp.int32, (GLA_HEADS * CHUNK, GLA_VAL), 0) // CHUNK
    cv = lax.broadcasted_iota(jnp.int32, (GLA_HEADS * CHUNK, GLA_VAL), 1) // GLA_DV
    bd_v = rv == cv
    bd_vb = bd_v.astype(jnp.bfloat16)

    st = s_ref[...]
    o_parts = []
    for c in range(nch):
        sl = slice(c * CHUNK, (c + 1) * CHUNK)
        att = jnp.zeros((CHUNK, GLA_HEADS * CHUNK), jnp.float32)
        for lvl in range(len(masks)):
            kbd = jnp.concatenate([kh[lvl][sl]] * GLA_HEADS, axis=0) * bd_k
            sc = lax.dot_general(qh[lvl][sl], kbd, (((1,), (1,)), ((), ())),
                                 preferred_element_type=jnp.float32)
            att = jnp.where(masks[lvl], sc, att)
        vbd = jnp.concatenate([vb[sl]] * GLA_HEADS, axis=0) * bd_vb
        o_parts.append(_dot(_bf(att), vbd) + _dot(qe[sl], _bf(st)))
        upd = lax.dot_general(kl[sl], vb[sl], (((0,), (0,)), ((), ())), preferred_element_type=jnp.float32)
        st = st * dl_t[:, c:c + 1] + jnp.where(bd_v, upd, 0.0)
    s_ref[...] = st
    o = jnp.concatenate(o_parts, axis=0)

    gnw = gnw_ref[...]
    heads = []
    for hd in range(GLA_HEADS):
        oh = o[:, hd * GLA_DV:(hd + 1) * GLA_DV]
        heads.append(_rms(oh, gnw))
    o = jnp.concatenate(heads, axis=-1) * (g * jax.nn.sigmoid(g))

    mix = _bf(jnp.concatenate([o, u], axis=-1))
    o_ref[0] = x + _dot(mix, wout_ref[...])


def _mixer_call(x, layer, ln_w, w_in_a, w_in_b, wgu_p, b_gate, gnw, conv_w, conv_b, cn_w, cn_b, w_out_b):
    bsz, seq, d = x.shape
    tt = min(TIME_TILE, seq)
    assert seq % tt == 0 and tt % CHUNK == 0
    const = lambda *shape: pl.BlockSpec((None,) + shape, lambda b, t: (layer,) + (0,) * len(shape))
    return pl.pallas_call(
        _mixer_kernel,
        out_shape=jax.ShapeDtypeStruct(x.shape, x.dtype),
        grid=(bsz, seq // tt),
        in_specs=[
            pl.BlockSpec((1, tt, d), lambda b, t: (b, t, 0)),
            const(1, d), const(d, IN_COLS_A), const(d, IN_COLS_B), const(GATE_PAD, GLA_KEY), const(1, GLA_KEY),
            const(1, GLA_DV),
            const(CONV_HALO, CONV_CH), const(1, CONV_CH), const(1, CONV_CH), const(1, CONV_CH),
            const(GLA_VAL + CONV_CH, d),
        ],
        out_specs=pl.BlockSpec((1, tt, d), lambda b, t: (b, t, 0)),
        scratch_shapes=[
            pltpu.VMEM((GLA_HEADS * GLA_DK, GLA_VAL), jnp.float32),
            pltpu.VMEM((CONV_HALO + tt + SUBLANES, CONV_CH), jnp.float32),
            pltpu.VMEM((SUBLANES // 2 - 1, CONV_HALO + tt, CONV_CH), jnp.float32),
            pltpu.VMEM((tt + SUBLANES, CONV_CH), jnp.float32),
            pltpu.VMEM((tt, CONV_CH), jnp.float32),
        ],
        compiler_params=pltpu.CompilerParams(dimension_semantics=("arbitrary", "arbitrary"),
                                             vmem_limit_bytes=VMEM_LIMIT),
        name="mixer",
    )(x, ln_w, w_in_a, w_in_b, wgu_p, b_gate, gnw, conv_w, conv_b, cn_w, cn_b, w_out_b)


def _swiglu_chunk(h, wg_ref, wu_ref, wd_ref, f0):
    gt = _dot(h, wg_ref[:, f0:f0 + FF_CHUNK])
    up = _dot(h, wu_ref[:, f0:f0 + FF_CHUNK])
    a = _bf(gt * jax.nn.sigmoid(gt) * up)
    return _dot(a, wd_ref[f0:f0 + FF_CHUNK, :])


def _swiglu(h, wg_ref, wu_ref, wd_ref):
    y = None
    for f0 in range(0, D_FF, FF_CHUNK):
        part = _swiglu_chunk(h, wg_ref, wu_ref, wd_ref, f0)
        y = part if y is None else y + part
    return y


class _WeightStream:
    def __init__(self, wg_hbm, wu_hbm, wd_hbm, wg_ref, wu_ref, wd_ref, sg_ref, su_ref, sd_ref, sems):
        self.hbm = (wg_hbm, wu_hbm, wd_hbm)
        self.dst = (wg_ref, wu_ref, wd_ref)
        self.stage = (sg_ref, su_ref, sd_ref)
        self.sems = sems
        gw = SWAP_GROUP * FF_CHUNK
        self.groups = [(c0, min(c0 + gw, D_FF)) for c0 in range(0, D_FF, gw)]

    def _copies(self, e, g):
        c0, c1 = self.groups[g]
        w, slot = c1 - c0, g % 2
        (wg_hbm, wu_hbm, wd_hbm), (sg_ref, su_ref, sd_ref) = self.hbm, self.stage
        sem = self.sems.at[slot]
        return (pltpu.make_async_copy(wg_hbm.at[e, :, pl.ds(c0, w)], sg_ref.at[slot, :, pl.ds(0, w)], sem),
                pltpu.make_async_copy(wu_hbm.at[e, :, pl.ds(c0, w)], su_ref.at[slot, :, pl.ds(0, w)], sem),
                pltpu.make_async_copy(wd_hbm.at[e, pl.ds(c0, w), :], sd_ref.at[slot, pl.ds(0, w), :], sem))

    def start(self, e, g):
        for cp in self._copies(e, g):
            cp.start()

    def install(self, e, g):
        for cp in self._copies(e, g):
            cp.wait()
        c0, c1 = self.groups[g]
        w, slot = c1 - c0, g % 2
        (wg_ref, wu_ref, wd_ref), (sg_ref, su_ref, sd_ref) = self.dst, self.stage
        wg_ref[:, c0:c1] = _bf(sg_ref[slot, :, 0:w])
        wu_ref[:, c0:c1] = _bf(su_ref[slot, :, 0:w])
        wd_ref[c0:c1, :] = _bf(sd_ref[slot, 0:w, :])

    def load_all(self, e):
        self.start(e, 0)
        for g in range(len(self.groups)):
            if g + 1 < len(self.groups):
                self.start(e, g + 1)
            self.install(e, g)


def _weight_scratch(d):
    gw = SWAP_GROUP * FF_CHUNK
    return [pltpu.VMEM((d, D_FF), jnp.bfloat16), pltpu.VMEM((d, D_FF), jnp.bfloat16),
            pltpu.VMEM((D_FF, d), jnp.bfloat16),
            pltpu.VMEM((2, d, gw), jnp.float32), pltpu.VMEM((2, d, gw), jnp.float32),
            pltpu.VMEM((2, gw, d), jnp.float32), pltpu.SemaphoreType.DMA((2,))]


def _ffn_dense_kernel(x_ref, ln_ref, fin_ref, wg_hbm, wu_hbm, wd_hbm, o_ref, wg_ref, wu_ref, wd_ref, sg_ref, su_ref,
                      sd_ref, sems, *, final_norm):
    @pl.when(pl.program_id(0) == 0)
    def _():
        _WeightStream(wg_hbm, wu_hbm, wd_hbm, wg_ref, wu_ref, wd_ref, sg_ref, su_ref, sd_ref, sems).load_all(0)

    x = x_ref[...]
    out = x + _swiglu(_bf(_rms(x, ln_ref[...])), wg_ref, wu_ref, wd_ref)
    if final_norm:
        out = _rms(out, fin_ref[...])
    o_ref[...] = out


def _ffn_dense_call(x2, ln_w, fin_w, wg, wu, wd, *, final_norm):
    n, d = x2.shape
    tm = min(ROW_TILE, n)
    assert n % tm == 0
    return pl.pallas_call(
        functools.partial(_ffn_dense_kernel, final_norm=final_norm),
        out_shape=jax.ShapeDtypeStruct(x2.shape, x2.dtype),
        grid=(n // tm,),
        in_specs=[
            pl.BlockSpec((tm, d), lambda i: (i, 0)),
            pl.BlockSpec((1, d), lambda i: (0, 0)),
            pl.BlockSpec((1, d), lambda i: (0, 0)),
            pl.BlockSpec(memory_space=pl.ANY),
            pl.BlockSpec(memory_space=pl.ANY),
            pl.BlockSpec(memory_space=pl.ANY),
        ],
        out_specs=pl.BlockSpec((tm, d), lambda i: (i, 0)),
        scratch_shapes=_weight_scratch(d),
        compiler_params=pltpu.CompilerParams(dimension_semantics=("arbitrary",), vmem_limit_bytes=VMEM_LIMIT),
        name="ffn_dense",
    )(x2, ln_w, fin_w, wg, wu, wd)


def _store_records(rec_ref, vals):
    m = vals.shape[0]
    for s in range(REC_ROWS):
        rec_ref[pl.ds(s, m, stride=REC_ROWS), :] = vals[:, s * LANES:(s + 1) * LANES]


def _load_records(rec_ref, m):
    return [rec_ref[pl.ds(s, m, stride=REC_ROWS), :] for s in range(REC_ROWS)]


def _copy_records(src_ref, src_off, dst_ref, dst_off, n, sem, max_rows, wait=False):
    bit = max_rows.bit_length() - 1
    while bit >= 0:
        size = (1 << bit) * REC_ROWS
        done = lax.shift_left(lax.shift_right_logical(n, bit + 1), bit + 1)
        src0 = 0 if src_off is None else pl.multiple_of((src_off + done) * REC_ROWS, REC_ROWS)
        dst0 = pl.multiple_of((dst_off + done) * REC_ROWS, REC_ROWS)

        @pl.when((lax.shift_right_logical(n, bit) & 1) == 1)
        def _(size=size, src0=src0, dst0=dst0, prio=bit % 2):
            cp = pltpu.make_async_copy(src_ref.at[pl.ds(src0, size)], dst_ref.at[pl.ds(dst0, size)], sem)
            cp.wait() if wait else cp.start(priority=prio)
        bit -= 1


def _one_hot_rows(pos0, pos1, n_rows):
    r = lax.broadcasted_iota(jnp.int32, (n_rows, pos0.shape[1]), 0)
    return jnp.where((r == pos0) | (r == pos1), 1.0, 0.0).astype(jnp.bfloat16)


def _route_kernel(x_ref, ln_ref, wrh_ref, wrl_ref, pos_ref, gate_ref, cnt_ref):
    ts = x_ref.shape[0]
    t = pl.program_id(0)
    h = _rms(x_ref[...], ln_ref[...])
    h_hi = _bf(h)
    h_lo = _bf(h - h_hi.astype(jnp.float32))
    logits = _dot(h_hi, wrh_ref[...]) + (_dot(h_hi, wrl_ref[...]) + _dot(h_lo, wrh_ref[...]))
    lt = jnp.transpose(logits)[0:N_EXPERTS, :]
    row = lax.broadcasted_iota(jnp.int32, lt.shape, 0)
    neg = jnp.float32(-jnp.inf)
    m1 = jnp.max(lt, axis=0, keepdims=True)
    i1 = jnp.min(jnp.where(lt == m1, row, N_EXPERTS), axis=0, keepdims=True)
    rest = jnp.where(row == i1, neg, lt)
    m2 = jnp.max(rest, axis=0, keepdims=True)
    i2 = jnp.min(jnp.where(rest == m2, row, N_EXPERTS), axis=0, keepdims=True)
    e2 = jnp.exp(m2 - m1)
    den = 1.0 + e2
    g1 = 1.0 / den
    g2 = e2 / den
    sel1 = row == i1
    sel2 = row == i2
    oh = jnp.where(sel1 | sel2, 1.0, 0.0)

    sp = lax.broadcasted_iota(jnp.int32, (ts, ts), 0)
    sc = lax.broadcasted_iota(jnp.int32, (ts, ts), 1)
    upper = jnp.where(sp < sc, 1.0, 0.0).astype(jnp.bfloat16)
    rank = _dot(_bf(oh), upper)
    rk1 = jnp.sum(jnp.where(sel1, rank, 0.0), axis=0, keepdims=True)
    rk2 = jnp.sum(jnp.where(sel2, rank, 0.0), axis=0, keepdims=True)

    cnts, offs = [], []
    off = jnp.int32(0)
    for e in range(N_EXPERTS):
        c = jnp.sum(oh[e:e + 1, :]).astype(jnp.int32)
        cnts.append(c)
        offs.append(off)
        off = off + c
    off1 = jnp.zeros_like(rk1)
    off2 = jnp.zeros_like(rk2)
    for e in range(N_EXPERTS):
        fe = offs[e].astype(jnp.float32)
        off1 = jnp.where(i1 == e, fe, off1)
        off2 = jnp.where(i2 == e, fe, off2)
    pos1 = (off1 + rk1).astype(jnp.int32)
    pos2 = (off2 + rk2).astype(jnp.int32)
    pos_ref[0] = jnp.concatenate([pos1, pos2, jnp.zeros((SUBLANES - TOP_K, ts), jnp.int32)], axis=0)
    gate_ref[0] = jnp.concatenate([g1, g2, jnp.zeros((SUBLANES - TOP_K, ts), jnp.float32)], axis=0)
    for e in range(N_EXPERTS):
        cnt_ref[t * N_EXPERTS + e] = cnts[e]


def _route_call(x2, ln_w, wr_hi, wr_lo):
    n, d = x2.shape
    ts = min(ROW_TILE, n)
    nt = n // ts
    return pl.pallas_call(
        _route_kernel,
        out_shape=(
            jax.ShapeDtypeStruct((nt, SUBLANES, ts), jnp.int32),
            jax.ShapeDtypeStruct((nt, SUBLANES, ts), jnp.float32),
            jax.ShapeDtypeStruct((nt * N_EXPERTS,), jnp.int32),
        ),
        grid=(nt,),
        in_specs=[
            pl.BlockSpec((ts, d), lambda i: (i, 0)),
            pl.BlockSpec((1, d), lambda i: (0, 0)),
            pl.BlockSpec((d, LANES), lambda i: (0, 0)),
            pl.BlockSpec((d, LANES), lambda i: (0, 0)),
        ],
        out_specs=(
            pl.BlockSpec((1, SUBLANES, ts), lambda i: (i, 0, 0)),
            pl.BlockSpec((1, SUBLANES, ts), lambda i: (i, 0, 0)),
            pl.BlockSpec(memory_space=pltpu.SMEM),
        ),
        compiler_params=pltpu.CompilerParams(dimension_semantics=("arbitrary",), vmem_limit_bytes=VMEM_LIMIT),
        name="route",
    )(x2, ln_w, wr_hi, wr_lo)


def _dispatch_kernel(seg_ref, cnt_ref, zstart_ref, zlen_ref, nval_ref, x_ref, ln_ref, pos_ref, xs_hbm,
                     stage_ref, zero_ref, sem, zsem, *, max_tiles):
    ts = x_ref.shape[0]
    n_sorted = TOP_K * ts
    t = pl.program_id(0)
    nt = pl.num_programs(0)
    h = _bf(_rms(x_ref[...], ln_ref[...]))
    pos = pos_ref[0]
    perm = _one_hot_rows(pos[0:1, :], pos[1:2, :], n_sorted)
    sorted_h = _dot(perm, h)

    @pl.when(t > 0)
    def _():
        pltpu.make_async_copy(stage_ref, xs_hbm.at[pl.ds(0, n_sorted * REC_ROWS)], sem).wait()

    _store_records(stage_ref, sorted_h)

    off = jnp.int32(0)
    for e in range(N_EXPERTS):
        c = cnt_ref[t * N_EXPERTS + e]
        _copy_records(stage_ref, off, xs_hbm, seg_ref[t * N_EXPERTS + e], c, sem, ts)
        off = off + c

    @pl.when(t == nt - 1)
    def _():
        pltpu.make_async_copy(stage_ref, xs_hbm.at[pl.ds(0, n_sorted * REC_ROWS)], sem).wait()
        tm = zero_ref.shape[0] // REC_ROWS
        zero_ref[...] = jnp.zeros_like(zero_ref)

        def tail_fill(k):
            return pltpu.make_async_copy(
                zero_ref, xs_hbm.at[pl.ds(pl.multiple_of((nval_ref[0] + k) * tm * REC_ROWS, REC_ROWS),
                                          tm * REC_ROWS)], zsem)

        for wait in (False, True):
            for e in range(N_EXPERTS):
                _copy_records(zero_ref, None, xs_hbm, zstart_ref[e], zlen_ref[e], zsem, tm, wait=wait)
            for k in range(N_EXPERTS):
                @pl.when(nval_ref[0] + k < max_tiles)
                def _(k=k, wait=wait):
                    tail_fill(k).wait() if wait else tail_fill(k).start()


def _dispatch_call(seg, cnt, zstart, zlen, n_valid, x2, ln_w, pos, max_tiles):
    n, d = x2.shape
    ts = min(ROW_TILE, n)
    nt = n // ts
    return pl.pallas_call(
        functools.partial(_dispatch_kernel, max_tiles=max_tiles),
        out_shape=jax.ShapeDtypeStruct((max_tiles * ROW_TILE * REC_ROWS, LANES), jnp.float32),
        grid_spec=pltpu.PrefetchScalarGridSpec(
            num_scalar_prefetch=5,
            grid=(nt,),
            in_specs=[
                pl.BlockSpec((ts, d), lambda i, *_: (i, 0)),
                pl.BlockSpec((1, d), lambda i, *_: (0, 0)),
                pl.BlockSpec((1, SUBLANES, ts), lambda i, *_: (i, 0, 0)),
            ],
            out_specs=pl.BlockSpec(memory_space=pl.ANY),
            scratch_shapes=[
                pltpu.VMEM((TOP_K * ts * REC_ROWS, LANES), jnp.float32),
                pltpu.VMEM((ROW_TILE * REC_ROWS, LANES), jnp.float32),
                pltpu.SemaphoreType.DMA(()),
                pltpu.SemaphoreType.DMA(()),
            ],
        ),
        compiler_params=pltpu.CompilerParams(dimension_semantics=("arbitrary",), vmem_limit_bytes=VMEM_LIMIT),
        name="dispatch",
    )(seg, cnt, zstart, zlen, n_valid, x2, ln_w, pos)


def _ffn_group_kernel(texp_ref, tnext_ref, tswap_ref, nval_ref, xs_ref, wg_hbm, wu_hbm, wd_hbm, ys_ref,
                      wg_ref, wu_ref, wd_ref, sg_ref, su_ref, sd_ref, sems):
    j = pl.program_id(0)
    n_chunks = D_FF // FF_CHUNK
    ws = _WeightStream(wg_hbm, wu_hbm, wd_hbm, wg_ref, wu_ref, wd_ref, sg_ref, su_ref, sd_ref, sems)
    groups, start, install = ws.groups, ws.start, ws.install

    def load_h():
        tm = xs_ref.shape[0] // REC_ROWS
        return jnp.concatenate([_bf(w) for w in _load_records(xs_ref, tm)], axis=1)

    @pl.when(j == 0)
    def _():
        ws.load_all(texp_ref[0])

    valid = j < nval_ref[0]
    swap = tswap_ref[j] == 1
    not_ = jnp.logical_not

    @pl.when(not_(valid))
    def _():
        ys_ref[...] = jnp.zeros_like(ys_ref)

    @pl.when(valid & not_(swap))
    def _():
        _store_records(ys_ref, _swiglu(load_h(), wg_ref, wu_ref, wd_ref))

    @pl.when(valid & swap)
    def _():
        e = tnext_ref[j]
        start(e, 0)
        start(e, 1)
        h = load_h()
        y = None
        for f in range(n_chunks):
            part = _swiglu_chunk(h, wg_ref, wu_ref, wd_ref, f * FF_CHUNK)
            y = part if y is None else y + part
            g = f // SWAP_GROUP
            if (f + 1) * FF_CHUNK == groups[g][1]:
                install(e, g)
                if g + 2 < len(groups):
                    start(e, g + 2)
        _store_records(ys_ref, y)


def _ffn_group_call(tile_expert, tile_next, tile_swap, n_valid, xs, wg, wu, wd):
    rows = xs.shape[0]
    tm = ROW_TILE
    n_tiles = tile_expert.shape[0]
    d = wg.shape[1]
    return pl.pallas_call(
        _ffn_group_kernel,
        out_shape=jax.ShapeDtypeStruct((rows, LANES), jnp.float32),
        grid_spec=pltpu.PrefetchScalarGridSpec(
            num_scalar_prefetch=4,
            grid=(n_tiles,),
            in_specs=[
                pl.BlockSpec((tm * REC_ROWS, LANES), lambda j, *_: (j, 0)),
                pl.BlockSpec(memory_space=pl.ANY),
                pl.BlockSpec(memory_space=pl.ANY),
                pl.BlockSpec(memory_space=pl.ANY),
            ],
            out_specs=pl.BlockSpec((tm * REC_ROWS, LANES), lambda j, *_: (j, 0)),
            scratch_shapes=_weight_scratch(d),
        ),
        compiler_params=pltpu.CompilerParams(dimension_semantics=("arbitrary",), vmem_limit_bytes=VMEM_LIMIT),
        name="ffn_group",
    )(tile_expert, tile_next, tile_swap, n_valid, xs, wg, wu, wd)


def _combine_kernel(seg_ref, cnt_ref, x_ref, fin_ref, pos_ref, gate_ref, ys_hbm, o_ref, ybuf_ref, sems,
                    *, final_norm):
    ts = x_ref.shape[0]
    n_sorted = TOP_K * ts
    t = pl.program_id(0)
    nt = pl.num_programs(0)

    def fetch(tile, slot):
        off = jnp.int32(0)
        for e in range(N_EXPERTS):
            c = cnt_ref[tile * N_EXPERTS + e]
            _copy_records(ys_hbm, seg_ref[tile * N_EXPERTS + e], ybuf_ref.at[slot], off, c, sems.at[slot], ts)
            off = off + c

    @pl.when(t == 0)
    def _():
        fetch(0, 0)

    @pl.when(t + 1 < nt)
    def _():
        fetch(t + 1, (t + 1) % 2)

    slot = t % 2
    pltpu.make_async_copy(ys_hbm.at[pl.ds(0, n_sorted * REC_ROWS)], ybuf_ref.at[slot], sems.at[slot]).wait()
    ys = jnp.concatenate([_bf(w) for w in _load_records(ybuf_ref.at[slot], n_sorted)], axis=1)
    pos = pos_ref[0]
    gates = gate_ref[0]
    r = lax.broadcasted_iota(jnp.int32, (n_sorted, ts), 0)
    sel = _bf(jnp.where(r == pos[0:1, :], gates[0:1, :], 0.0) + jnp.where(r == pos[1:2, :], gates[1:2, :], 0.0))
    y = lax.dot_general(sel, ys, (((0,), (0,)), ((), ())), preferred_element_type=jnp.float32)
    out = x_ref[...] + y
    if final_norm:
        out = _rms(out, fin_ref[...])
    o_ref[...] = out


def _combine_call(seg, cnt, x2, fin_w, pos, gates, ys, *, final_norm):
    n, d = x2.shape
    ts = min(ROW_TILE, n)
    nt = n // ts
    return pl.pallas_call(
        functools.partial(_combine_kernel, final_norm=final_norm),
        out_shape=jax.ShapeDtypeStruct(x2.shape, x2.dtype),
        grid_spec=pltpu.PrefetchScalarGridSpec(
            num_scalar_prefetch=2,
            grid=(nt,),
            in_specs=[
                pl.BlockSpec((ts, d), lambda i, sg, ct: (i, 0)),
                pl.BlockSpec((1, d), lambda i, sg, ct: (0, 0)),
                pl.BlockSpec((1, SUBLANES, ts), lambda i, sg, ct: (i, 0, 0)),
                pl.BlockSpec((1, SUBLANES, ts), lambda i, sg, ct: (i, 0, 0)),
                pl.BlockSpec(memory_space=pl.ANY),
            ],
            out_specs=pl.BlockSpec((ts, d), lambda i, sg, ct: (i, 0)),
            scratch_shapes=[
                pltpu.VMEM((2, TOP_K * ts * REC_ROWS, LANES), jnp.float32),
                pltpu.SemaphoreType.DMA((2,)),
            ],
        ),
        compiler_params=pltpu.CompilerParams(dimension_semantics=("arbitrary",), vmem_limit_bytes=VMEM_LIMIT),
        name="combine",
    )(seg, cnt, x2, fin_w, pos, gates, ys)


def _moe(x2, ln_w, fin_w, w_router, wg, wu, wd, *, final_norm):
    n, d = x2.shape
    tm = ROW_TILE
    wr_p = jnp.concatenate([w_router, jnp.zeros((d, LANES - N_EXPERTS), w_router.dtype)], axis=1)
    wr_hi = _bf(wr_p)
    wr_lo = _bf(wr_p - wr_hi.astype(jnp.float32))
    pos, gates, cnt = _route_call(x2, ln_w, wr_hi, wr_lo)

    i32 = jnp.int32
    nt = cnt.shape[0] // N_EXPERTS
    cnt2 = cnt.reshape(nt, N_EXPERTS)
    totals = jnp.sum(cnt2, axis=0)
    tiles_e = (totals + tm - 1) // tm
    ends = jnp.cumsum(tiles_e)
    base = (ends - tiles_e) * tm
    seg = (base[None, :] + jnp.cumsum(cnt2, axis=0) - cnt2).reshape(-1).astype(i32)
    zstart = (base + totals).astype(i32)
    zlen = (tiles_e * tm - totals).astype(i32)
    n_valid = ends[-1].reshape(1).astype(i32)
    max_tiles = (TOP_K * n) // tm + N_EXPERTS
    j = jnp.minimum(jnp.arange(max_tiles, dtype=i32), n_valid - 1)
    tile_expert = jnp.sum((j[:, None] >= ends[None, :]).astype(i32), axis=1).astype(i32)
    tile_next = jnp.concatenate([tile_expert[1:], tile_expert[-1:]])
    tile_swap = (tile_next != tile_expert).astype(i32)

    xs = _dispatch_call(seg, cnt, zstart, zlen, n_valid, x2, ln_w, pos, max_tiles)
    ys = _ffn_group_call(tile_expert, tile_next, tile_swap, n_valid, xs, wg, wu, wd)
    return _combine_call(seg, cnt, x2, fin_w, pos, gates, ys, final_norm=final_norm)


def _prep_w_in(w):
    depth, d, _ = w.shape
    parts = [w[:, :, IN_COLS_A:IN_COLS_A + GATE_RANK], jnp.zeros((depth, d, GATE_PAD - GATE_RANK), w.dtype),
             w[:, :, IN_COLS_A + GATE_RANK:]]
    return _bf(w[:, :, :IN_COLS_A]), _bf(jnp.concatenate(parts, axis=2))


def kernel(x, ln1_w, w_in, w_gate_up, b_gate, gla_norm_w, conv_w, conv_b, cn_w, cn_b, w_out, ln2_w, wd_gate,
           wd_up, wd_down, w_router, we_gate, we_up, we_down, final_norm_w):
    bsz, seq, d = x.shape
    depth = ln1_w.shape[0]
    n = bsz * seq
    row = lambda a: a.reshape(1, -1)
    rows = lambda a: a[:, None, :]
    w_in_a, w_in_b = _prep_w_in(w_in)
    wgu_p = _bf(jnp.concatenate(
        [w_gate_up, jnp.zeros((depth, GATE_PAD - GATE_RANK, GLA_KEY), w_gate_up.dtype)], axis=1))
    cw_p = jnp.concatenate([conv_w, jnp.zeros((depth, CONV_HALO - CONV_WIDTH, CONV_CH), conv_w.dtype)], axis=1)
    w_out_b = _bf(w_out)
    for l in range(depth):
        x = _mixer_call(x, l, rows(ln1_w), w_in_a, w_in_b, wgu_p, rows(b_gate), rows(gla_norm_w), cw_p, rows(conv_b),
                        rows(cn_w), rows(cn_b), w_out_b)
        x2 = x.reshape(n, d)
        last = l == depth - 1
        i = l // 2
        if l % 2 == 0:
            x2 = _ffn_dense_call(x2, row(ln2_w[l]), row(final_norm_w), wd_gate[i:i + 1], wd_up[i:i + 1],
                                 wd_down[i:i + 1], final_norm=last)
        else:
            x2 = _moe(x2, row(ln2_w[l]), row(final_norm_w), w_router[i], we_gate[i], we_up[i], we_down[i],
                      final_norm=last)
        x = x2.reshape(bsz, seq, d)
    return x
```
